```python
import math
import jax, jax.numpy as jnp
from jax import lax
import numpy as np

D_MODEL = 2048
BATCH = 1
SEQ = 8192
DEPTH = 4

MIX_WIDTH = D_MODEL
FOURIER_WIDTH = D_MODEL // 2
N_FOURIER_GROUPS = 8
FOURIER_GROUP_DIM = FOURIER_WIDTH // N_FOURIER_GROUPS
ATTN_WIDTH = MIX_WIDTH - FOURIER_WIDTH
N_HEADS = 8
V_HEAD_DIM = ATTN_WIDTH // N_HEADS
QK_HEAD_DIM = V_HEAD_DIM // 2
IN_WIDTH = FOURIER_WIDTH + 3 * ATTN_WIDTH
D_FF = 5632
ROPE_THETA = 10000.0
Q_BLOCK = 128
NORM_EPS = 1e-6
SUBLN_EPS = 1e-5
LAMBDA_STD = 0.1

kernel_name = "hybrid_fourier_diffattn_macaron_encoder"


def _rmsnorm(x, g, eps=NORM_EPS):
    xf = x.astype(jnp.float32)
    y = xf * lax.rsqrt(jnp.mean(xf * xf, axis=-1, keepdims=True) + eps)
    return (y * g.astype(jnp.float32)).astype(x.dtype)


def _swiglu(h, w_gate, w_up, w_down):
    return (jax.nn.silu(h @ w_gate) * (h @ w_up)) @ w_down


def _rope_tables(positions):
    inv_freq = 1.0 / (ROPE_THETA ** (jnp.arange(0, QK_HEAD_DIM, 2, dtype=jnp.float32) / QK_HEAD_DIM))
    ang = positions.astype(jnp.float32)[..., None] * inv_freq
    ang = jnp.concatenate([ang, ang], axis=-1)
    return jnp.cos(ang), jnp.sin(ang)


def _rope(t, cos, sin):
    tf = t.astype(jnp.float32)
    c = cos[:, :, None, None, :]
    s = sin[:, :, None, None, :]
    half = QK_HEAD_DIM // 2
    rot = jnp.concatenate([-tf[..., half:], tf[..., :half]], axis=-1)
    return tf * c + rot * s


def _fourier_mix(u):
    b, s, _ = u.shape
    ug = u.reshape(b, s, N_FOURIER_GROUPS, FOURIER_GROUP_DIM).astype(jnp.float32)
    y = jnp.fft.fft2(ug, axes=(1, 3), norm="ortho").real
    return y.reshape(b, s, FOURIER_WIDTH).astype(u.dtype)


def _diff_attention(q, k, v, cos, sin, g_q, g_k, lq1, lk1, lq2, lk2, g_sub, layer_idx):
    b, s, _ = q.shape
    q = q.reshape(b, s, N_HEADS, 2, QK_HEAD_DIM)
    k = k.reshape(b, s, N_HEADS, 2, QK_HEAD_DIM)
    v = v.reshape(b, s, N_HEADS, V_HEAD_DIM)
    scale = QK_HEAD_DIM ** -0.5
    q = _rope(_rmsnorm(q, g_q), cos, sin) * scale
    k = _rope(_rmsnorm(k, g_k), cos, sin)
    vf = v.astype(jnp.float32)

    lambda_init = 0.8 - 0.6 * math.exp(-0.3 * layer_idx)
    lam = (jnp.exp(jnp.sum(lq1.astype(jnp.float32) * lk1.astype(jnp.float32)))
           - jnp.exp(jnp.sum(lq2.astype(jnp.float32) * lk2.astype(jnp.float32)))
           + lambda_init)

    n_blocks = s // Q_BLOCK
    qb = q.reshape(b, n_blocks, Q_BLOCK, N_HEADS, 2, QK_HEAD_DIM).transpose(1, 0, 2, 3, 4, 5)

    def block(q_blk):
        scores = jnp.einsum('bqhcd,bkhcd->bhcqk', q_blk, k)
        p = jax.nn.softmax(scores, axis=-1)
        a = p[:, :, 0] - lam * p[:, :, 1]
        return jnp.einsum('bhqk,bkhd->bqhd', a, vf)

    o = lax.map(block, qb)
    o = o.transpose(1, 0, 2, 3, 4).reshape(b, s, N_HEADS, V_HEAD_DIM)
    o = _rmsnorm(o, g_sub, SUBLN_EPS) * (1.0 - lambda_init)
    return o.reshape(b, s, ATTN_WIDTH).astype(v.dtype)


def setup_inputs(seed: int = 0) -> dict:
    key = jax.random.key(seed)
    ks = jax.random.split(key, 24)
    f32 = jnp.float32

    def w(k, shape, fan_in):
        return jax.random.normal(k, shape, f32) * (fan_in ** -0.5)

    def gain(k, shape):
        return 1.0 + 0.02 * jax.random.normal(k, shape, f32)

    x = jax.random.normal(ks[0], (BATCH, SEQ, D_MODEL), f32)
    positions = jnp.broadcast_to(jnp.arange(SEQ, dtype=jnp.int32)[None, :], (BATCH, SEQ))
    return {
        "x": x,
        "positions": positions,
        "norm_ffn1": gain(ks[1], (DEPTH, D_MODEL)),
        "w1_gate": w(ks[2], (DEPTH, D_MODEL, D_FF), D_MODEL),
        "w1_up": w(ks[3], (DEPTH, D_MODEL, D_FF), D_MODEL),
        "w1_down": w(ks[4], (DEPTH, D_FF, D_MODEL), D_FF),
        "norm_mix": gain(ks[5], (DEPTH, D_MODEL)),
        "w_in": w(ks[6], (DEPTH, D_MODEL, IN_WIDTH), D_MODEL),
        "q_norm": gain(ks[7], (DEPTH, QK_HEAD_DIM)),
        "k_norm": gain(ks[8], (DEPTH, QK_HEAD_DIM)),
        "lambda_q1": LAMBDA_STD * jax.random.normal(ks[9], (DEPTH, QK_HEAD_DIM), f32),
        "lambda_k1": LAMBDA_STD * jax.random.normal(ks[10], (DEPTH, QK_HEAD_DIM), f32),
        "lambda_q2": LAMBDA_STD * jax.random.normal(ks[11], (DEPTH, QK_HEAD_DIM), f32),
        "lambda_k2": LAMBDA_STD * jax.random.normal(ks[12], (DEPTH, QK_HEAD_DIM), f32),
        "subln": gain(ks[13], (DEPTH, V_HEAD_DIM)),
        "w_out": w(ks[14], (DEPTH, MIX_WIDTH, D_MODEL), MIX_WIDTH),
        "norm_ffn2": gain(ks[15], (DEPTH, D_MODEL)),
        "w2_gate": w(ks[16], (DEPTH, D_MODEL, D_FF), D_MODEL),
        "w2_up": w(ks[17], (DEPTH, D_MODEL, D_FF), D_MODEL),
        "w2_down": w(ks[18], (DEPTH, D_FF, D_MODEL), D_FF),
    }


def reference(x, positions, norm_ffn1, w1_gate, w1_up, w1_down, norm_mix, w_in,
              q_norm, k_norm, lambda_q1, lambda_k1, lambda_q2, lambda_k2, subln,
              w_out, norm_ffn2, w2_gate, w2_up, w2_down):
    cos, sin = _rope_tables(positions)
    split_at = [FOURIER_WIDTH, FOURIER_WIDTH + ATTN_WIDTH, FOURIER_WIDTH + 2 * ATTN_WIDTH]
    for l in range(DEPTH):
        x = x + 0.5 * _swiglu(_rmsnorm(x, norm_ffn1[l]), w1_gate[l], w1_up[l], w1_down[l])
        h = _rmsnorm(x, norm_mix[l])
        proj = h @ w_in[l]
        u_f, q, k, v = jnp.split(proj, split_at, axis=-1)
        y_f = _fourier_mix(u_f)
        y_a = _diff_attention(q, k, v, cos, sin, q_norm[l], k_norm[l],
                              lambda_q1[l], lambda_k1[l], lambda_q2[l], lambda_k2[l],
                              subln[l], l)
        x = x + jnp.concatenate([y_f, y_a], axis=-1) @ w_out[l]
        x = x + 0.5 * _swiglu(_rmsnorm(x, norm_ffn2[l]), w2_gate[l], w2_up[l], w2_down[l])
    return x
```

```python
import functools
import math

import numpy as np
import jax
import jax.numpy as jnp
from jax import lax
from jax.experimental import pallas as pl
from jax.experimental.pallas import tpu as pltpu

F32 = jnp.float32
BF16 = jnp.bfloat16

N_HEADS = 8
V_HEAD_DIM = 128
QK_HEAD_DIM = 64
N_FOURIER_GROUPS = 8
FOURIER_GROUP_DIM = 128
ROPE_THETA = 10000.0
NORM_EPS = 1e-6
SUBLN_EPS = 1e-5
LOG2_E = 1.4426950408889634

V7X_VMEM_BYTES = 64 * 1024 * 1024
VMEM_LIMIT_BYTES = V7X_VMEM_BYTES - 8 * 1024 * 1024
LANES = 128

ROW_TILE = 512
FF_TILE = 512
Q_TILE = 512
KV_TILE = 512
DFT_S2 = 128
F1_S2_TILE = 8
M_INIT = -1e30


def _params(*semantics):
    return pltpu.CompilerParams(dimension_semantics=semantics, vmem_limit_bytes=VMEM_LIMIT_BYTES)


def _rms_rows(x, g):
    ms = jnp.mean(x * x, axis=-1, keepdims=True)
    return x * lax.rsqrt(ms + NORM_EPS) * g


def _ffn_body(x_ref, g_ref, wg_ref, wu_ref, wd_ref, o_ref, xn_ref):
    @pl.when(pl.program_id(1) == 0)
    def _():
        x = x_ref[...]
        xn_ref[...] = _rms_rows(x, g_ref[...]).astype(BF16)
        o_ref[...] = x

    xn = xn_ref[...]
    h = jnp.dot(xn, wg_ref[...], preferred_element_type=F32)
    u = jnp.dot(xn, wu_ref[...], preferred_element_type=F32)
    a = (h * jax.nn.sigmoid(h) * u).astype(BF16)
    o_ref[...] += 0.5 * jnp.dot(a, wd_ref[...], preferred_element_type=F32)


def _ffn(x, g, wg, wu, wd, layer):
    s, d = x.shape
    f = wg.shape[-1]
    tm, tf = min(ROW_TILE, s), min(FF_TILE, f)
    return pl.pallas_call(
        _ffn_body,
        grid=(s // tm, f // tf),
        in_specs=[
            pl.BlockSpec((tm, d), lambda i, j: (i, 0)),
            pl.BlockSpec((1, d), lambda i, j: (0, 0)),
            pl.BlockSpec((None, d, tf), lambda i, j: (layer, 0, j)),
            pl.BlockSpec((None, d, tf), lambda i, j: (layer, 0, j)),
            pl.BlockSpec((None, tf, d), lambda i, j: (layer, j, 0)),
        ],
        out_specs=pl.BlockSpec((tm, d), lambda i, j: (i, 0)),
        out_shape=jax.ShapeDtypeStruct((s, d), F32),
        scratch_shapes=[pltpu.VMEM((tm, d), BF16)],
        compiler_params=_params("parallel", "arbitrary"),
        name="ffn",
    )(x, g, wg, wu, wd)


def _rope_body(pos_ref, invf_ref, cos_ref, sina_ref, sinb_ref):
    ang = pos_ref[...] * invf_ref[...]
    c = jnp.cos(ang)
    s = jnp.sin(ang)
    lane = lax.broadcasted_iota(jnp.int32, ang.shape, 1)
    first = (lane & (QK_HEAD_DIM // 2)) == 0
    cos_ref[...] = c
    sina_ref[...] = jnp.where(first, -s, 0.0)
    sinb_ref[...] = jnp.where(first, 0.0, s)


def _rope_tables(positions):
    s = positions.shape[0]
    inv_freq = 1.0 / (ROPE_THETA ** (jnp.arange(0, QK_HEAD_DIM, 2, dtype=F32) / QK_HEAD_DIM))
    invf = jnp.tile(inv_freq, 2 * V_HEAD_DIM // QK_HEAD_DIM).reshape(1, V_HEAD_DIM)
    pos = positions.astype(F32).reshape(s, 1)
    ts = min(1024, s)
    tab = jax.ShapeDtypeStruct((s, V_HEAD_DIM), F32)
    return pl.pallas_call(
        _rope_body,
        grid=(s // ts,),
        in_specs=[pl.BlockSpec((ts, 1), lambda i: (i, 0)), pl.BlockSpec((1, V_HEAD_DIM), lambda i: (0, 0))],
        out_specs=[pl.BlockSpec((ts, V_HEAD_DIM), lambda i: (i, 0))] * 3,
        out_shape=[tab, tab, tab],
        compiler_params=_params("parallel"),
        name="rope_tables",
    )(pos, invf)


def _group_mean_matrix():
    idx = np.arange(V_HEAD_DIM) // QK_HEAD_DIM
    return jnp.asarray((idx[:, None] == idx[None, :]).astype(np.float32) / QK_HEAD_DIM, dtype=BF16)


def _qk_norm_rope(t, gain, gmat, cos, sina, sinb):
    sq = t * t
    hi = sq.astype(BF16)
    lo = (sq - hi.astype(F32)).astype(BF16)
    ms = jnp.dot(hi, gmat, preferred_element_type=F32) + jnp.dot(lo, gmat, preferred_element_type=F32)
    tn = t * lax.rsqrt(ms + NORM_EPS) * gain
    half = QK_HEAD_DIM // 2
    return tn * cos + pltpu.roll(tn, V_HEAD_DIM - half, 1) * sina + pltpu.roll(tn, half, 1) * sinb


def _proj_f_body(x_ref, g_ref, w_ref, o_ref):
    xn = _rms_rows(x_ref[...], g_ref[...]).astype(BF16)
    o_ref[...] = jnp.dot(xn, w_ref[...], preferred_element_type=F32).astype(BF16)


def _proj_k_body(x_ref, g_ref, w_ref, gain_ref, gmat_ref, cos_ref, sina_ref, sinb_ref, o_ref):
    xn = _rms_rows(x_ref[...], g_ref[...]).astype(BF16)
    p = jnp.dot(xn, w_ref[...], preferred_element_type=F32)
    for h in range(N_HEADS):
        sl = slice(h * V_HEAD_DIM, (h + 1) * V_HEAD_DIM)
        t = _qk_norm_rope(p[:, sl], gain_ref[...], gmat_ref[...], cos_ref[...], sina_ref[...], sinb_ref[...])
        o_ref[:, sl] = t.astype(BF16)


def _proj_q_body(x_ref, g_ref, w_ref, gain_ref, gmat_ref, cos_ref, sina_ref, sinb_ref, o_ref, *, scale):
    xn = _rms_rows(x_ref[...], g_ref[...]).astype(BF16)
    p = jnp.dot(xn, w_ref[...], preferred_element_type=F32)
    for h in range(N_HEADS):
        sl = slice(h * V_HEAD_DIM, (h + 1) * V_HEAD_DIM)
        t = _qk_norm_rope(p[:, sl], gain_ref[...], gmat_ref[...], cos_ref[...], sina_ref[...], sinb_ref[...])
        o_ref[h] = (t * scale).T.astype(BF16)


def _proj_v_body(x_ref, g_ref, w_ref, o_ref):
    xn = _rms_rows(x_ref[...], g_ref[...]).astype(BF16)
    p = jnp.dot(xn, w_ref[...], preferred_element_type=F32)
    for h in range(N_HEADS):
        o_ref[h, 0] = p[:, h * V_HEAD_DIM:(h + 1) * V_HEAD_DIM].T.astype(BF16)


def _proj(mode, x, g, w_in, layer, col_block, extras=(), scale=None):
    s, d = x.shape
    n = N_HEADS * V_HEAD_DIM
    tm = min(ROW_TILE, s)
    in_specs = [
        pl.BlockSpec((tm, d), lambda i: (i, 0)),
        pl.BlockSpec((1, d), lambda i: (0, 0)),
        pl.BlockSpec((None, d, n), lambda i: (layer, 0, col_block)),
    ]
    if mode in ("q", "k"):
        gain, gmat, cos, sina, sinb = extras
        in_specs += [
            pl.BlockSpec((1, V_HEAD_DIM), lambda i: (0, 0)),
            pl.BlockSpec((V_HEAD_DIM, V_HEAD_DIM), lambda i: (0, 0)),
            pl.BlockSpec((tm, V_HEAD_DIM), lambda i: (i, 0)),
            pl.BlockSpec((tm, V_HEAD_DIM), lambda i: (i, 0)),
            pl.BlockSpec((tm, V_HEAD_DIM), lambda i: (i, 0)),
        ]
    if mode == "f":
        body, out_shape = _proj_f_body, jax.ShapeDtypeStruct((s, n), BF16)
        out_spec = pl.BlockSpec((tm, n), lambda i: (i, 0))
    elif mode == "k":
        body, out_shape = _proj_k_body, jax.ShapeDtypeStruct((s, n), BF16)
        out_spec = pl.BlockSpec((tm, n), lambda i: (i, 0))
    elif mode == "q":
        body = functools.partial(_proj_q_body, scale=scale)
        out_shape = jax.ShapeDtypeStruct((N_HEADS, V_HEAD_DIM, s), BF16)
        out_spec = pl.BlockSpec((N_HEADS, V_HEAD_DIM, tm), lambda i: (0, 0, i))
    else:
        assert mode == "v" and tm == KV_TILE
        body = _proj_v_body
        out_shape = jax.ShapeDtypeStruct((N_HEADS, s // tm, V_HEAD_DIM, tm), BF16)
        out_spec = pl.BlockSpec((N_HEADS, 1, V_HEAD_DIM, tm), lambda i: (0, i, 0, 0))
    return pl.pallas_call(
        body,
        grid=(s // tm,),
        in_specs=in_specs,
        out_specs=out_spec,
        out_shape=out_shape,
        compiler_params=_params("parallel"),
        name="proj_" + mode,
    )(x, g, w_in, *extras)


def _dft_tables(s):
    n1, n2 = s // DFT_S2, DFT_S2
    a1 = 2.0 * np.pi * np.outer(np.arange(n1), np.arange(n1)) / n1
    cs1 = np.concatenate([np.cos(a1), -np.sin(a1)], axis=0)
    at = 2.0 * np.pi * np.outer(np.arange(n1), np.arange(n2)) / s
    nt = n2 // F1_S2_TILE
    twc = np.cos(at).reshape(n1, nt, F1_S2_TILE).transpose(1, 0, 2)
    tws = np.sin(at).reshape(n1, nt, F1_S2_TILE).transpose(1, 0, 2)
    a2 = 2.0 * np.pi * np.outer(np.arange(n2), np.arange(n2)) / n2
    c2, s2 = np.cos(a2), np.sin(a2)
    f2 = np.block([[c2, s2], [-s2, c2]])
    ac = 2.0 * np.pi * np.outer(np.arange(FOURIER_GROUP_DIM), np.arange(FOURIER_GROUP_DIM)) / FOURIER_GROUP_DIM
    cc = np.concatenate([np.cos(ac), np.sin(ac)], axis=0)
    return (jnp.asarray(cs1, dtype=BF16), jnp.asarray(twc, dtype=F32), jnp.asarray(tws, dtype=F32),
            jnp.asarray(f2, dtype=BF16), jnp.asarray(cc, dtype=BF16))


def _f1_body(u_ref, cs_ref, twc_ref, tws_ref, z_ref, *, n1, ncol):
    zc = jnp.dot(cs_ref[...], u_ref[...], preferred_element_type=F32)
    twc = twc_ref[0]
    tws = tws_ref[0]
    for j in range(F1_S2_TILE):
        zr = zc[0:n1, j * ncol:(j + 1) * ncol]
        zi = zc[n1:2 * n1, j * ncol:(j + 1) * ncol]
        c = twc[:, j:j + 1]
        s = tws[:, j:j + 1]
        z_ref[0, j] = (zr * c + zi * s).astype(BF16)
        z_ref[1, j] = (zi * c - zr * s).astype(BF16)


def _f2_body(z_ref, f_ref, cc_ref, y_ref, *, norm):
    n2 = DFT_S2
    gd = FOURIER_GROUP_DIM
    zz = z_ref[...].reshape(2 * n2, z_ref.shape[-1])
    xb = jnp.dot(f_ref[...], zz, preferred_element_type=F32).astype(BF16)
    for g in range(N_FOURIER_GROUPS):
        lhs = jnp.concatenate([xb[0:n2, g * gd:(g + 1) * gd], xb[n2:2 * n2, g * gd:(g + 1) * gd]], axis=1)
        y = jnp.dot(lhs, cc_ref[...], preferred_element_type=F32)
        y_ref[:, g * gd:(g + 1) * gd] = (y * norm).astype(BF16)


def _fourier_mix(u, tables):
    s, ncol = u.shape
    n1, n2 = s // DFT_S2, DFT_S2
    cs1, twc, tws, f2, cc = tables
    ts2 = F1_S2_TILE
    z = pl.pallas_call(
        functools.partial(_f1_body, n1=n1, ncol=ncol),
        grid=(n2 // ts2,),
        in_specs=[
            pl.BlockSpec((n1, ts2 * ncol), lambda t: (0, t)),
            pl.BlockSpec((2 * n1, n1), lambda t: (0, 0)),
            pl.BlockSpec((1, n1, ts2), lambda t: (t, 0, 0)),
            pl.BlockSpec((1, n1, ts2), lambda t: (t, 0, 0)),
        ],
        out_specs=pl.BlockSpec((2, ts2, n1, ncol), lambda t: (0, t, 0, 0)),
        out_shape=jax.ShapeDtypeStruct((2, n2, n1, ncol), BF16),
        compiler_params=_params("parallel"),
        name="dft_stage1",
    )(u.reshape(n1, n2 * ncol), cs1, twc, tws)
    norm = 1.0 / math.sqrt(s * FOURIER_GROUP_DIM)
    y = pl.pallas_call(
        functools.partial(_f2_body, norm=norm),
        grid=(n1,),
        in_specs=[
            pl.BlockSpec((2, n2, ncol), lambda k: (0, 0, k)),
            pl.BlockSpec((2 * n2, 2 * n2), lambda k: (0, 0)),
            pl.BlockSpec((2 * FOURIER_GROUP_DIM, FOURIER_GROUP_DIM), lambda k: (0, 0)),
        ],
        out_specs=pl.BlockSpec((n2, ncol), lambda k: (0, k)),
        out_shape=jax.ShapeDtypeStruct((n2, n1 * ncol), BF16),
        compiler_params=_params("parallel"),
        name="dft_stage2",
    )(z.reshape(2, n2, n1 * ncol), f2, cc)
    return y.reshape(s, ncol)


def _attn_body(qt_ref, k_ref, vt_ref, lq1_ref, lk1_ref, lq2_ref, lk2_ref, gsub_ref, o_ref,
               m_ref, l_ref, acc_ref, *, lambda_init):
    nk, tk = vt_ref.shape[1], vt_ref.shape[3]
    qt = qt_ref[0]
    row = lax.broadcasted_iota(jnp.int32, qt.shape, 0)
    zero = jnp.zeros_like(qt)
    q_halves = (jnp.where(row < QK_HEAD_DIM, qt, zero), jnp.where(row >= QK_HEAD_DIM, qt, zero))
    m_ref[...] = jnp.full(m_ref.shape, M_INIT, F32)
    l_ref[...] = jnp.zeros(l_ref.shape, F32)
    acc_ref[...] = jnp.zeros(acc_ref.shape, F32)

    def chunk(c, carry):
        off = pl.multiple_of(c * tk, tk)
        kc = k_ref[pl.ds(off, tk), :]
        vc = vt_ref[0, c]
        for hf in range(2):
            s = jnp.dot(kc, q_halves[hf], preferred_element_type=F32)
            m_old = m_ref[hf]
            m_new = jnp.maximum(m_old, jnp.max(s, axis=0, keepdims=True))
            alpha = jnp.exp2(m_old - m_new)
            p = jnp.exp2(s - m_new)
            l_ref[hf] = alpha * l_ref[hf] + jnp.sum(p, axis=0, keepdims=True)
            acc_ref[hf] = alpha * acc_ref[hf] + jnp.dot(vc, p.astype(BF16), preferred_element_type=F32)
            m_ref[hf] = m_new
        return carry

    lax.fori_loop(0, nk, chunk, 0)

    lam = (jnp.exp(jnp.sum(lq1_ref[...] * lk1_ref[...], axis=-1, keepdims=True))
           - jnp.exp(jnp.sum(lq2_ref[...] * lk2_ref[...], axis=-1, keepdims=True)) + lambda_init)
    o = acc_ref[0] / l_ref[0] - lam * (acc_ref[1] / l_ref[1])
    ms = jnp.mean(o * o, axis=0, keepdims=True)
    y = o * lax.rsqrt(ms + SUBLN_EPS) * gsub_ref[...] * (1.0 - lambda_init)
    o_ref[...] = y.T.astype(BF16)


def _diff_attention(qt, k, vt, lq1, lk1, lq2, lk2, gsub, lambda_init):
    h, _, s = qt.shape
    nk, tk = vt.shape[1], vt.shape[3]
    tq = min(Q_TILE, s)
    vec = pl.BlockSpec((1, QK_HEAD_DIM), lambda hh, i: (0, 0))
    return pl.pallas_call(
        functools.partial(_attn_body, lambda_init=lambda_init),
        grid=(h, s // tq),
        in_specs=[
            pl.BlockSpec((1, V_HEAD_DIM, tq), lambda hh, i: (hh, 0, i)),
            pl.BlockSpec((s, V_HEAD_DIM), lambda hh, i: (0, hh)),
            pl.BlockSpec((1, nk, V_HEAD_DIM, tk), lambda hh, i: (hh, 0, 0, 0)),
            vec, vec, vec, vec,
            pl.BlockSpec((V_HEAD_DIM, 1), lambda hh, i: (0, 0)),
        ],
        out_specs=pl.BlockSpec((tq, V_HEAD_DIM), lambda hh, i: (i, hh)),
        out_shape=jax.ShapeDtypeStruct((s, h * V_HEAD_DIM), BF16),
        scratch_shapes=[
            pltpu.VMEM((2, 1, tq), F32),
            pltpu.VMEM((2, 1, tq), F32),
            pltpu.VMEM((2, V_HEAD_DIM, tq), F32),
        ],
        compiler_params=_params("parallel", "parallel"),
        name="diff_attention",
    )(qt, k, vt, lq1, lk1, lq2, lk2, gsub)


def _outproj_body(x_ref, yf_ref, ya_ref, wf_ref, wa_ref, o_ref):
    o_ref[...] = (x_ref[...]
                  + jnp.dot(yf_ref[...], wf_ref[...], preferred_element_type=F32)
                  + jnp.dot(ya_ref[...], wa_ref[...], preferred_element_type=F32))


def _outproj(x, yf, ya, w_out, layer):
    s, d = x.shape
    nf, na = yf.shape[1], ya.shape[1]
    assert nf == na
    tm = min(ROW_TILE, s)
    return pl.pallas_call(
        _outproj_body,
        grid=(s // tm,),
        in_specs=[
            pl.BlockSpec((tm, d), lambda i: (i, 0)),
            pl.BlockSpec((tm, nf), lambda i: (i, 0)),
            pl.BlockSpec((tm, na), lambda i: (i, 0)),
            pl.BlockSpec((None, nf, d), lambda i: (layer, 0, 0)),
            pl.BlockSpec((None, na, d), lambda i: (layer, 1, 0)),
        ],
        out_specs=pl.BlockSpec((tm, d), lambda i: (i, 0)),
        out_shape=jax.ShapeDtypeStruct((s, d), F32),
        compiler_params=_params("parallel"),
        name="out_proj",
    )(x, yf, ya, w_out, w_out)


def kernel(x, positions, norm_ffn1, w1_gate, w1_up, w1_down, norm_mix, w_in, q_norm, k_norm,
           lambda_q1, lambda_k1, lambda_q2, lambda_k2, subln, w_out, norm_ffn2, w2_gate, w2_up, w2_down):
    b, s, d = x.shape
    depth = w_in.shape[0]
    assert b == 1 and s % DFT_S2 == 0
    xs = x.reshape(s, d)
    w1g, w1u, w1d = w1_gate.astype(BF16), w1_up.astype(BF16), w1_down.astype(BF16)
    w2g, w2u, w2d = w2_gate.astype(BF16), w2_up.astype(BF16), w2_down.astype(BF16)
    win, wout = w_in.astype(BF16), w_out.astype(BF16)

    cos, sina, sinb = _rope_tables(positions.reshape(s))
    gmat = _group_mean_matrix()
    tables = _dft_tables(s)
    q_scale = (QK_HEAD_DIM ** -0.5) * LOG2_E

    for l in range(depth):
        lambda_init = 0.8 - 0.6 * math.exp(-0.3 * l)
        xs = _ffn(xs, norm_ffn1[l].reshape(1, d), w1g, w1u, w1d, l)

        g_mix = norm_mix[l].reshape(1, d)
        gq = jnp.tile(q_norm[l], 2).reshape(1, V_HEAD_DIM)
        gk = jnp.tile(k_norm[l], 2).reshape(1, V_HEAD_DIM)
        u_f = _proj("f", xs, g_mix, win, l, 0)
        qt = _proj("q", xs, g_mix, win, l, 1, (gq, gmat, cos, sina, sinb), scale=q_scale)
        kk = _proj("k", xs, g_mix, win, l, 2, (gk, gmat, cos, sina, sinb))
        vt = _proj("v", xs, g_mix, win, l, 3)
        y_f = _fourier_mix(u_f, tables)
        y_a = _diff_attention(qt, kk, vt,
                              lambda_q1[l].reshape(1, -1), lambda_k1[l].reshape(1, -1),
                              lambda_q2[l].reshape(1, -1), lambda_k2[l].reshape(1, -1),
                              subln[l].reshape(V_HEAD_DIM, 1), lambda_init)
        xs = _outproj(xs, y_f, y_a, wout, l)

        xs = _ffn(xs, norm_ffn2[l].reshape(1, d), w2g, w2u, w2d, l)
    return xs.reshape(b, s, d)
```

```python
import functools
import math

import numpy as np
import jax
import jax.numpy as jnp
from jax import lax
from jax.experimental import pallas as pl
from jax.experimental.pallas import tpu as pltpu

F32 = jnp.float32
BF16 = jnp.bfloat16

N_HEADS = 8
V_HEAD_DIM = 128
QK_HEAD_DIM = 64
N_FOURIER_GROUPS = 8
FOURIER_GROUP_DIM = 128
ROPE_THETA = 10000.0
NORM_EPS = 1e-6
SUBLN_EPS = 1e-5
LOG2_E = 1.4426950408889634

V7X_VMEM_BYTES = 64 * 1024 * 1024
VMEM_LIMIT_BYTES = V7X_VMEM_BYTES - 8 * 1024 * 1024
LANES = 128

ROW_TILE = 512
FF_TILE = 512
Q_TILE = 512
KV_TILE = 512
DFT_S2 = 128
F1_S2_TILE = 8
KV_UNROLL = 16
KV_UNIT = 256
QK_LOOKAHEAD = 2
M_INIT = -1e30
MAX_BOUNDED_SOFTMAX_LOG2 = 60.0


def _params(*semantics):
    return pltpu.CompilerParams(dimension_semantics=semantics, vmem_limit_bytes=VMEM_LIMIT_BYTES)


def _rms_rows(x, g):
    ms = jnp.mean(x * x, axis=-1, keepdims=True)
    return x * lax.rsqrt(ms + NORM_EPS) * g


def _ffn_body(x_ref, g_ref, wg_ref, wu_ref, wd_ref, o_ref, xn_ref):
    @pl.when(pl.program_id(1) == 0)
    def _():
        x = x_ref[...]
        xn_ref[...] = _rms_rows(x, g_ref[...]).astype(BF16)
        o_ref[...] = x

    xn = xn_ref[...]
    h = jnp.dot(xn, wg_ref[...], preferred_element_type=F32)
    u = jnp.dot(xn, wu_ref[...], preferred_element_type=F32)
    a = (h * jax.nn.sigmoid(h) * u).astype(BF16)
    o_ref[...] += 0.5 * jnp.dot(a, wd_ref[...], preferred_element_type=F32)


def _ffn(x, g, wg, wu, wd, layer):
    s, d = x.shape
    f = wg.shape[-1]
    tm, tf = min(ROW_TILE, s), min(FF_TILE, f)
    return pl.pallas_call(
        _ffn_body,
        grid=(s // tm, f // tf),
        in_specs=[
            pl.BlockSpec((tm, d), lambda i, j: (i, 0)),
            pl.BlockSpec((1, d), lambda i, j: (0, 0)),
            pl.BlockSpec((None, d, tf), lambda i, j: (layer, 0, j)),
            pl.BlockSpec((None, d, tf), lambda i, j: (layer, 0, j)),
            pl.BlockSpec((None, tf, d), lambda i, j: (layer, j, 0)),
        ],
        out_specs=pl.BlockSpec((tm, d), lambda i, j: (i, 0)),
        out_shape=jax.ShapeDtypeStruct((s, d), F32),
        scratch_shapes=[pltpu.VMEM((tm, d), BF16)],
        compiler_params=_params("parallel", "arbitrary"),
        name="ffn",
    )(x, g, wg, wu, wd)


def _rope_body(pos_ref, invf_ref, cos_ref, sina_ref, sinb_ref):
    ang = pos_ref[...] * invf_ref[...]
    c = jnp.cos(ang)
    s = jnp.sin(ang)
    lane = lax.broadcasted_iota(jnp.int32, ang.shape, 1)
    first = (lane & (QK_HEAD_DIM // 2)) == 0
    cos_ref[...] = c
    sina_ref[...] = jnp.where(first, -s, 0.0)
    sinb_ref[...] = jnp.where(first, 0.0, s)


def _rope_tables(positions):
    s = positions.shape[0]
    inv_freq = 1.0 / (ROPE_THETA ** (jnp.arange(0, QK_HEAD_DIM, 2, dtype=F32) / QK_HEAD_DIM))
    invf = jnp.tile(inv_freq, 2 * V_HEAD_DIM // QK_HEAD_DIM).reshape(1, V_HEAD_DIM)
    pos = positions.astype(F32).reshape(s, 1)
    ts = min(1024, s)
    tab = jax.ShapeDtypeStruct((s, V_HEAD_DIM), F32)
    return pl.pallas_call(
        _rope_body,
        grid=(s // ts,),
        in_specs=[pl.BlockSpec((ts, 1), lambda i: (i, 0)), pl.BlockSpec((1, V_HEAD_DIM), lambda i: (0, 0))],
        out_specs=[pl.BlockSpec((ts, V_HEAD_DIM), lambda i: (i, 0))] * 3,
        out_shape=[tab, tab, tab],
        compiler_params=_params("parallel"),
        name="rope_tables",
    )(pos, invf)


def _group_mean_matrix():
    idx = np.arange(V_HEAD_DIM) // QK_HEAD_DIM
    return jnp.asarray((idx[:, None] == idx[None, :]).astype(np.float32) / QK_HEAD_DIM, dtype=BF16)


def _qk_norm_rope(t, gain, gmat, cos, sina, sinb):
    sq = t * t
    hi = sq.astype(BF16)
    lo = (sq - hi.astype(F32)).astype(BF16)
    ms = jnp.dot(hi, gmat, preferred_element_type=F32) + jnp.dot(lo, gmat, preferred_element_type=F32)
    tn = t * lax.rsqrt(ms + NORM_EPS) * gain
    half = QK_HEAD_DIM // 2
    return tn * cos + pltpu.roll(tn, V_HEAD_DIM - half, 1) * sina + pltpu.roll(tn, half, 1) * sinb


def _proj_f_body(x_ref, g_ref, w_ref, o_ref):
    xn = _rms_rows(x_ref[...], g_ref[...]).astype(BF16)
    o_ref[...] = jnp.dot(xn, w_ref[...], preferred_element_type=F32).astype(BF16)


def _proj_k_body(x_ref, g_ref, w_ref, gain_ref, gmat_ref, cos_ref, sina_ref, sinb_ref, o_ref):
    xn = _rms_rows(x_ref[...], g_ref[...]).astype(BF16)
    p = jnp.dot(xn, w_ref[...], preferred_element_type=F32)
    for h in range(N_HEADS):
        sl = slice(h * V_HEAD_DIM, (h + 1) * V_HEAD_DIM)
        t = _qk_norm_rope(p[:, sl], gain_ref[...], gmat_ref[...], cos_ref[...], sina_ref[...], sinb_ref[...])
        o_ref[:, sl] = t.astype(BF16)


def _proj_q_body(x_ref, g_ref, w_ref, gain_ref, gmat_ref, cos_ref, sina_ref, sinb_ref, o_ref, *, scale):
    xn = _rms_rows(x_ref[...], g_ref[...]).astype(BF16)
    p = jnp.dot(xn, w_ref[...], preferred_element_type=F32)
    for h in range(N_HEADS):
        sl = slice(h * V_HEAD_DIM, (h + 1) * V_HEAD_DIM)
        t = _qk_norm_rope(p[:, sl], gain_ref[...], gmat_ref[...], cos_ref[...], sina_ref[...], sinb_ref[...])
        o_ref[h] = (t * scale).T.astype(BF16)


def _proj_v_body(x_ref, g_ref, w_ref, o_ref):
    xn = _rms_rows(x_ref[...], g_ref[...]).astype(BF16)
    p = jnp.dot(xn, w_ref[...], preferred_element_type=F32)
    for h in range(N_HEADS):
        o_ref[h, 0] = p[:, h * V_HEAD_DIM:(h + 1) * V_HEAD_DIM].T.astype(BF16)


def _proj(mode, x, g, w_in, layer, col_block, extras=(), scale=None):
    s, d = x.shape
    n = N_HEADS * V_HEAD_DIM
    tm = min(ROW_TILE, s)
    in_specs = [
        pl.BlockSpec((tm, d), lambda i: (i, 0)),
        pl.BlockSpec((1, d), lambda i: (0, 0)),
        pl.BlockSpec((None, d, n), lambda i: (layer, 0, col_block)),
    ]
    if mode in ("q", "k"):
        gain, gmat, cos, sina, sinb = extras
        in_specs += [
            pl.BlockSpec((1, V_HEAD_DIM), lambda i: (0, 0)),
            pl.BlockSpec((V_HEAD_DIM, V_HEAD_DIM), lambda i: (0, 0)),
            pl.BlockSpec((tm, V_HEAD_DIM), lambda i: (i, 0)),
            pl.BlockSpec((tm, V_HEAD_DIM), lambda i: (i, 0)),
            pl.BlockSpec((tm, V_HEAD_DIM), lambda i: (i, 0)),
        ]
    if mode == "f":
        body, out_shape = _proj_f_body, jax.ShapeDtypeStruct((s, n), BF16)
        out_spec = pl.BlockSpec((tm, n), lambda i: (i, 0))
    elif mode == "k":
        body, out_shape = _proj_k_body, jax.ShapeDtypeStruct((s, n), BF16)
        out_spec = pl.BlockSpec((tm, n), lambda i: (i, 0))
    elif mode == "q":
        body = functools.partial(_proj_q_body, scale=scale)
        out_shape = jax.ShapeDtypeStruct((N_HEADS, V_HEAD_DIM, s), BF16)
        out_spec = pl.BlockSpec((N_HEADS, V_HEAD_DIM, tm), lambda i: (0, 0, i))
    else:
        assert mode == "v" and tm == KV_TILE
        body = _proj_v_body
        out_shape = jax.ShapeDtypeStruct((N_HEADS, s // tm, V_HEAD_DIM, tm), BF16)
        out_spec = pl.BlockSpec((N_HEADS, 1, V_HEAD_DIM, tm), lambda i: (0, i, 0, 0))
    return pl.pallas_call(
        body,
        grid=(s // tm,),
        in_specs=in_specs,
        out_specs=out_spec,
        out_shape=out_shape,
        compiler_params=_params("parallel"),
        name="proj_" + mode,
    )(x, g, w_in, *extras)


def _dft_tables(s):
    n1, n2 = s // DFT_S2, DFT_S2
    a1 = 2.0 * np.pi * np.outer(np.arange(n1), np.arange(n1)) / n1
    cs1 = np.concatenate([np.cos(a1), -np.sin(a1)], axis=0)
    at = 2.0 * np.pi * np.outer(np.arange(n1), np.arange(n2)) / s
    nt = n2 // F1_S2_TILE
    twc = np.cos(at).reshape(n1, nt, F1_S2_TILE).transpose(1, 0, 2)
    tws = np.sin(at).reshape(n1, nt, F1_S2_TILE).transpose(1, 0, 2)
    a2 = 2.0 * np.pi * np.outer(np.arange(n2), np.arange(n2)) / n2
    c2, s2 = np.cos(a2), np.sin(a2)
    f2 = np.block([[c2, s2], [-s2, c2]])
    ac = 2.0 * np.pi * np.outer(np.arange(FOURIER_GROUP_DIM), np.arange(FOURIER_GROUP_DIM)) / FOURIER_GROUP_DIM
    cc = np.concatenate([np.cos(ac), np.sin(ac)], axis=0)
    return (jnp.asarray(cs1, dtype=BF16), jnp.asarray(twc, dtype=F32), jnp.asarray(tws, dtype=F32),
            jnp.asarray(f2, dtype=BF16), jnp.asarray(cc, dtype=BF16))


def _f1_body(u_ref, cs_ref, twc_ref, tws_ref, z_ref, *, n1, ncol):
    zc = jnp.dot(cs_ref[...], u_ref[...], preferred_element_type=F32)
    twc = twc_ref[0]
    tws = tws_ref[0]
    for j in range(F1_S2_TILE):
        zr = zc[0:n1, j * ncol:(j + 1) * ncol]
        zi = zc[n1:2 * n1, j * ncol:(j + 1) * ncol]
        c = twc[:, j:j + 1]
        s = tws[:, j:j + 1]
        z_ref[0, j] = (zr * c + zi * s).astype(BF16)
        z_ref[1, j] = (zi * c - zr * s).astype(BF16)


def _f2_body(z_ref, f_ref, cc_ref, y_ref, *, norm):
    n2 = DFT_S2
    gd = FOURIER_GROUP_DIM
    zz = z_ref[...].reshape(2 * n2, z_ref.shape[-1])
    xb = jnp.dot(f_ref[...], zz, preferred_element_type=F32).astype(BF16)
    for g in range(N_FOURIER_GROUPS):
        lhs = jnp.concatenate([xb[0:n2, g * gd:(g + 1) * gd], xb[n2:2 * n2, g * gd:(g + 1) * gd]], axis=1)
        y = jnp.dot(lhs, cc_ref[...], preferred_element_type=F32)
        y_ref[:, g * gd:(g + 1) * gd] = (y * norm).astype(BF16)


def _fourier_mix(u, tables):
    s, ncol = u.shape
    n1, n2 = s // DFT_S2, DFT_S2
    cs1, twc, tws, f2, cc = tables
    ts2 = F1_S2_TILE
    z = pl.pallas_call(
        functools.partial(_f1_body, n1=n1, ncol=ncol),
        grid=(n2 // ts2,),
        in_specs=[
            pl.BlockSpec((n1, ts2 * ncol), lambda t: (0, t)),
            pl.BlockSpec((2 * n1, n1), lambda t: (0, 0)),
            pl.BlockSpec((1, n1, ts2), lambda t: (t, 0, 0)),
            pl.BlockSpec((1, n1, ts2), lambda t: (t, 0, 0)),
        ],
        out_specs=pl.BlockSpec((2, ts2, n1, ncol), lambda t: (0, t, 0, 0)),
        out_shape=jax.ShapeDtypeStruct((2, n2, n1, ncol), BF16),
        compiler_params=_params("parallel"),
        name="dft_stage1",
    )(u.reshape(n1, n2 * ncol), cs1, twc, tws)
    norm = 1.0 / math.sqrt(s * FOURIER_GROUP_DIM)
    y = pl.pallas_call(
        functools.partial(_f2_body, norm=norm),
        grid=(n1,),
        in_specs=[
            pl.BlockSpec((2, n2, ncol), lambda k: (0, 0, k)),
            pl.BlockSpec((2 * n2, 2 * n2), lambda k: (0, 0)),
            pl.BlockSpec((2 * FOURIER_GROUP_DIM, FOURIER_GROUP_DIM), lambda k: (0, 0)),
        ],
        out_specs=pl.BlockSpec((n2, ncol), lambda k: (0, k)),
        out_shape=jax.ShapeDtypeStruct((n2, n1 * ncol), BF16),
        compiler_params=_params("parallel"),
        name="dft_stage2",
    )(z.reshape(2, n2, n1 * ncol), f2, cc)
    return y.reshape(s, ncol)


def _score_bound(gq, gk):
    bq = jnp.max(jnp.abs(gq), axis=-1, keepdims=True)
    bk = jnp.max(jnp.abs(gk), axis=-1, keepdims=True)
    return (QK_HEAD_DIM * (QK_HEAD_DIM ** -0.5) * LOG2_E) * bq * bk


def _attn_prologue(qt_ref):
    qt = qt_ref[0]
    row = lax.broadcasted_iota(jnp.int32, qt.shape, 0)
    zero = jnp.zeros_like(qt)
    return jnp.where(row < QK_HEAD_DIM, qt, zero), jnp.where(row >= QK_HEAD_DIM, qt, zero)


def _attn_epilogue(o1, o2, lq1_ref, lk1_ref, lq2_ref, lk2_ref, gsub_ref, o_ref, lambda_init):
    lam = (jnp.exp(jnp.sum(lq1_ref[...] * lk1_ref[...], axis=-1, keepdims=True))
           - jnp.exp(jnp.sum(lq2_ref[...] * lk2_ref[...], axis=-1, keepdims=True)) + lambda_init)
    o = o1 - lam * o2
    ms = jnp.mean(o * o, axis=0, keepdims=True)
    y = o * lax.rsqrt(ms + SUBLN_EPS) * gsub_ref[...] * (1.0 - lambda_init)
    o_ref[...] = y.T.astype(BF16)


def _attn_online_body(qt_ref, k_ref, vt_ref, gq_ref, gk_ref, lq1_ref, lk1_ref, lq2_ref, lk2_ref, gsub_ref, o_ref,
                      m_ref, l_ref, acc_ref, *, lambda_init):
    del gq_ref, gk_ref
    nk, tk = vt_ref.shape[1], vt_ref.shape[3]
    q_halves = _attn_prologue(qt_ref)
    m_ref[...] = jnp.full(m_ref.shape, M_INIT, F32)
    l_ref[...] = jnp.zeros(l_ref.shape, F32)
    acc_ref[...] = jnp.zeros(acc_ref.shape, F32)

    def chunk(c, carry):
        off = pl.multiple_of(c * tk, tk)
        kc = k_ref[pl.ds(off, tk), :]
        vc = vt_ref[0, c]
        for hf in range(2):
            s = jnp.dot(kc, q_halves[hf], preferred_element_type=F32)
            m_old = m_ref[hf]
            m_new = jnp.maximum(m_old, jnp.max(s, axis=0, keepdims=True))
            alpha = jnp.exp2(m_old - m_new)
            p = jnp.exp2(s - m_new)
            l_ref[hf] = alpha * l_ref[hf] + jnp.sum(p, axis=0, keepdims=True)
            acc_ref[hf] = alpha * acc_ref[hf] + jnp.dot(vc, p.astype(BF16), preferred_element_type=F32)
            m_ref[hf] = m_new
        return carry

    lax.fori_loop(0, nk, chunk, 0)
    _attn_epilogue(acc_ref[0] / l_ref[0], acc_ref[1] / l_ref[1],
                   lq1_ref, lk1_ref, lq2_ref, lk2_ref, gsub_ref, o_ref, lambda_init)


def _attn_bounded_body(qt_ref, k_ref, vt_ref, gq_ref, gk_ref, lq1_ref, lk1_ref, lq2_ref, lk2_ref, gsub_ref, o_ref,
                       l_ref, acc_ref, *, lambda_init):
    nk, tk = vt_ref.shape[1], vt_ref.shape[3]
    tq = qt_ref.shape[2]
    q_halves = _attn_prologue(qt_ref)
    bound = _score_bound(gq_ref[...], gk_ref[...])
    l_ref[...] = jnp.zeros(l_ref.shape, F32)
    acc_ref[...] = jnp.zeros(acc_ref.shape, F32)

    per_chunk = tk // KV_UNIT
    unroll = math.gcd(nk, KV_UNROLL)
    units = [(cc, j, hf) for cc in range(unroll) for j in range(per_chunk) for hf in range(2)]

    def trip(t, carry):
        def scores(cc, j, hf):
            off = pl.multiple_of((t * unroll + cc) * tk + j * KV_UNIT, KV_UNIT)
            return jnp.dot(k_ref[pl.ds(off, KV_UNIT), :], q_halves[hf], preferred_element_type=F32)

        pending = [scores(*u) for u in units[:QK_LOOKAHEAD]]
        for n, (cc, j, hf) in enumerate(units):
            s = pending.pop(0)
            if n + QK_LOOKAHEAD < len(units):
                pending.append(scores(*units[n + QK_LOOKAHEAD]))
            p = jnp.exp2(s - bound)
            l_ref[hf] += jnp.sum(p.reshape(KV_UNIT // 8, 8, tq), axis=0)
            vc = vt_ref[0, t * unroll + cc, :, j * KV_UNIT:(j + 1) * KV_UNIT]
            acc_ref[hf] += jnp.dot(vc, p.astype(BF16), preferred_element_type=F32)
        return carry

    lax.fori_loop(0, nk // unroll, trip, 0)
    l1 = jnp.sum(l_ref[0], axis=0, keepdims=True)
    l2 = jnp.sum(l_ref[1], axis=0, keepdims=True)
    _attn_epilogue(acc_ref[0] / l1, acc_ref[1] / l2,
                   lq1_ref, lk1_ref, lq2_ref, lk2_ref, gsub_ref, o_ref, lambda_init)


def _diff_attention(qt, k, vt, gq, gk, lq1, lk1, lq2, lk2, gsub, lambda_init, online):
    h, _, s = qt.shape
    nk, tk = vt.shape[1], vt.shape[3]
    tq = min(Q_TILE, s)
    vec = pl.BlockSpec((1, QK_HEAD_DIM), lambda hh, i: (0, 0))
    if online:
        body = _attn_online_body
        scratch = [pltpu.VMEM((2, 1, tq), F32), pltpu.VMEM((2, 1, tq), F32), pltpu.VMEM((2, V_HEAD_DIM, tq), F32)]
    else:
        body = _attn_bounded_body
        scratch = [pltpu.VMEM((2, 8, tq), F32), pltpu.VMEM((2, V_HEAD_DIM, tq), F32)]
    return pl.pallas_call(
        functools.partial(body, lambda_init=lambda_init),
        grid=(h, s // tq),
        in_specs=[
            pl.BlockSpec((1, V_HEAD_DIM, tq), lambda hh, i: (hh, 0, i)),
            pl.BlockSpec((s, V_HEAD_DIM), lambda hh, i: (0, hh)),
            pl.BlockSpec((1, nk, V_HEAD_DIM, tk), lambda hh, i: (hh, 0, 0, 0)),
            vec, vec, vec, vec, vec, vec,
            pl.BlockSpec((V_HEAD_DIM, 1), lambda hh, i: (0, 0)),
        ],
        out_specs=pl.BlockSpec((tq, V_HEAD_DIM), lambda hh, i: (i, hh)),
        out_shape=jax.ShapeDtypeStruct((s, h * V_HEAD_DIM), BF16),
        scratch_shapes=scratch,
        compiler_params=_params("parallel", "parallel"),
        name="diff_attention_online" if online else "diff_attention_bounded",
    )(qt, k, vt, gq, gk, lq1, lk1, lq2, lk2, gsub)


def _outproj_body(x_ref, yf_ref, ya_ref, wf_ref, wa_ref, o_ref):
    o_ref[...] = (x_ref[...]
                  + jnp.dot(yf_ref[...], wf_ref[...], preferred_element_type=F32)
                  + jnp.dot(ya_ref[...], wa_ref[...], preferred_element_type=F32))


def _outproj(x, yf, ya, w_out, layer):
    s, d = x.shape
    nf, na = yf.shape[1], ya.shape[1]
    assert nf == na
    tm = min(ROW_TILE, s)
    return pl.pallas_call(
        _outproj_body,
        grid=(s // tm,),
        in_specs=[
            pl.BlockSpec((tm, d), lambda i: (i, 0)),
            pl.BlockSpec((tm, nf), lambda i: (i, 0)),
            pl.BlockSpec((tm, na), lambda i: (i, 0)),
            pl.BlockSpec((None, nf, d), lambda i: (layer, 0, 0)),
            pl.BlockSpec((None, na, d), lambda i: (layer, 1, 0)),
        ],
        out_specs=pl.BlockSpec((tm, d), lambda i: (i, 0)),
        out_shape=jax.ShapeDtypeStruct((s, d), F32),
        compiler_params=_params("parallel"),
        name="out_proj",
    )(x, yf, ya, w_out, w_out)


def kernel(x, positions, norm_ffn1, w1_gate, w1_up, w1_down, norm_mix, w_in, q_norm, k_norm,
           lambda_q1, lambda_k1, lambda_q2, lambda_k2, subln, w_out, norm_ffn2, w2_gate, w2_up, w2_down):
    b, s, d = x.shape
    depth = w_in.shape[0]
    assert b == 1 and s % DFT_S2 == 0
    xs = x.reshape(s, d)
    w1g, w1u, w1d = w1_gate.astype(BF16), w1_up.astype(BF16), w1_down.astype(BF16)
    w2g, w2u, w2d = w2_gate.astype(BF16), w2_up.astype(BF16), w2_down.astype(BF16)
    win, wout = w_in.astype(BF16), w_out.astype(BF16)

    cos, sina, sinb = _rope_tables(positions.reshape(s))
    gmat = _group_mean_matrix()
    tables = _dft_tables(s)
    q_scale = (QK_HEAD_DIM ** -0.5) * LOG2_E

    for l in range(depth):
        lambda_init = 0.8 - 0.6 * math.exp(-0.3 * l)
        xs = _ffn(xs, norm_ffn1[l].reshape(1, d), w1g, w1u, w1d, l)

        g_mix = norm_mix[l].reshape(1, d)
        gq = jnp.tile(q_norm[l], 2).reshape(1, V_HEAD_DIM)
        gk = jnp.tile(k_norm[l], 2).reshape(1, V_HEAD_DIM)
        u_f = _proj("f", xs, g_mix, win, l, 0)
        qt = _proj("q", xs, g_mix, win, l, 1, (gq, gmat, cos, sina, sinb), scale=q_scale)
        kk = _proj("k", xs, g_mix, win, l, 2, (gk, gmat, cos, sina, sinb))
        vt = _proj("v", xs, g_mix, win, l, 3)
        y_f = _fourier_mix(u_f, tables)
        gq64, gk64 = q_norm[l].reshape(1, QK_HEAD_DIM), k_norm[l].reshape(1, QK_HEAD_DIM)
        attn_args = (qt, kk, vt, gq64, gk64,
                     lambda_q1[l].reshape(1, -1), lambda_k1[l].reshape(1, -1),
                     lambda_q2[l].reshape(1, -1), lambda_k2[l].reshape(1, -1),
                     subln[l].reshape(V_HEAD_DIM, 1))
        y_a = lax.cond(
            _score_bound(gq64, gk64)[0, 0] <= MAX_BOUNDED_SOFTMAX_LOG2,
            lambda *a: _diff_attention(*a, lambda_init, False),
            lambda *a: _diff_attention(*a, lambda_init, True),
            *attn_args)
        xs = _outproj(xs, y_f, y_a, wout, l)

        xs = _ffn(xs, norm_ffn2[l].reshape(1, d), w2g, w2u, w2d, l)
    return xs.reshape(b, s, d)
```

```python
import functools
import math

import numpy as np
import jax
import jax.numpy as jnp
from jax import lax
from jax.experimental import pallas as pl
from jax.experimental.pallas import tpu as pltpu

F32 = jnp.float32
BF16 = jnp.bfloat16

N_HEADS = 8
V_HEAD_DIM = 128
QK_HEAD_DIM = 64
N_FOURIER_GROUPS = 8
FOURIER_GROUP_DIM = 128
ROPE_THETA = 10000.0
NORM_EPS = 1e-6
SUBLN_EPS = 1e-5
LOG2_E = 1.4426950408889634

V7X_VMEM_BYTES = 64 * 1024 * 1024
VMEM_LIMIT_BYTES = V7X_VMEM_BYTES - 8 * 1024 * 1024
LANES = 128

ROW_TILE = 512
FFN_ROW_TILE = 1024
FF_TILE = 256
Q_TILE = 512
KV_TILE = 512
DFT_S2 = 128
F1_S2_TILE = 8
KV_UNROLL = 16
KV_UNIT = 256
QK_LOOKAHEAD = 2
M_INIT = -1e30
MAX_BOUNDED_SOFTMAX_LOG2 = 60.0


def _params(*semantics):
    return pltpu.CompilerParams(dimension_semantics=semantics, vmem_limit_bytes=VMEM_LIMIT_BYTES)


def _rms_rows(x, g):
    ms = jnp.mean(x * x, axis=-1, keepdims=True)
    return x * lax.rsqrt(ms + NORM_EPS) * g


def _ffn_body(x_ref, g_ref, wg_ref, wu_ref, wd_ref, o_ref, xn_ref):
    @pl.when(pl.program_id(1) == 0)
    def _():
        x = x_ref[...]
        xn_ref[...] = _rms_rows(x, g_ref[...]).astype(BF16)
        o_ref[...] = x

    xn = xn_ref[...]
    h = jnp.dot(xn, wg_ref[...].astype(BF16), preferred_element_type=F32)
    u = jnp.dot(xn, wu_ref[...].astype(BF16), preferred_element_type=F32)
    a = (h * jax.nn.sigmoid(h) * u).astype(BF16)
    o_ref[...] += 0.5 * jnp.dot(a, wd_ref[...].astype(BF16), preferred_element_type=F32)


def _ffn(x, g, wg, wu, wd, layer):
    s, d = x.shape
    f = wg.shape[-1]
    tm, tf = min(FFN_ROW_TILE, s), min(FF_TILE, f)
    return pl.pallas_call(
        _ffn_body,
        grid=(s // tm, f // tf),
        in_specs=[
            pl.BlockSpec((tm, d), lambda i, j: (i, 0)),
            pl.BlockSpec((1, d), lambda i, j: (0, 0)),
            pl.BlockSpec((None, d, tf), lambda i, j: (layer, 0, j)),
            pl.BlockSpec((None, d, tf), lambda i, j: (layer, 0, j)),
            pl.BlockSpec((None, tf, d), lambda i, j: (layer, j, 0)),
        ],
        out_specs=pl.BlockSpec((tm, d), lambda i, j: (i, 0)),
        out_shape=jax.ShapeDtypeStruct((s, d), F32),
        scratch_shapes=[pltpu.VMEM((tm, d), BF16)],
        compiler_params=_params("parallel", "arbitrary"),
        name="ffn",
    )(x, g, wg, wu, wd)


def _rope_body(pos_ref, invf_ref, cos_ref, sina_ref, sinb_ref):
    ang = pos_ref[...] * invf_ref[...]
    c = jnp.cos(ang)
    s = jnp.sin(ang)
    lane = lax.broadcasted_iota(jnp.int32, ang.shape, 1)
    first = (lane & (QK_HEAD_DIM // 2)) == 0
    cos_ref[...] = c
    sina_ref[...] = jnp.where(first, -s, 0.0)
    sinb_ref[...] = jnp.where(first, 0.0, s)


def _rope_tables(positions):
    s = positions.shape[0]
    inv_freq = 1.0 / (ROPE_THETA ** (jnp.arange(0, QK_HEAD_DIM, 2, dtype=F32) / QK_HEAD_DIM))
    invf = jnp.tile(inv_freq, 2 * V_HEAD_DIM // QK_HEAD_DIM).reshape(1, V_HEAD_DIM)
    pos = positions.astype(F32).reshape(s, 1)
    ts = min(1024, s)
    tab = jax.ShapeDtypeStruct((s, V_HEAD_DIM), F32)
    return pl.pallas_call(
        _rope_body,
        grid=(s // ts,),
        in_specs=[pl.BlockSpec((ts, 1), lambda i: (i, 0)), pl.BlockSpec((1, V_HEAD_DIM), lambda i: (0, 0))],
        out_specs=[pl.BlockSpec((ts, V_HEAD_DIM), lambda i: (i, 0))] * 3,
        out_shape=[tab, tab, tab],
        compiler_params=_params("parallel"),
        name="rope_tables",
    )(pos, invf)


def _group_mean_matrix():
    idx = np.arange(V_HEAD_DIM) // QK_HEAD_DIM
    return jnp.asarray((idx[:, None] == idx[None, :]).astype(np.float32) / QK_HEAD_DIM, dtype=BF16)


def _qk_norm_rope(t, gain, gmat, cos, sina, sinb):
    sq = t * t
    hi = sq.astype(BF16)
    lo = (sq - hi.astype(F32)).astype(BF16)
    ms = jnp.dot(hi, gmat, preferred_element_type=F32) + jnp.dot(lo, gmat, preferred_element_type=F32)
    tn = t * lax.rsqrt(ms + NORM_EPS) * gain
    half = QK_HEAD_DIM // 2
    return tn * cos + pltpu.roll(tn, V_HEAD_DIM - half, 1) * sina + pltpu.roll(tn, half, 1) * sinb


def _proj_f_body(x_ref, g_ref, w_ref, o_ref):
    xn = _rms_rows(x_ref[...], g_ref[...]).astype(BF16)
    o_ref[...] = jnp.dot(xn, w_ref[...], preferred_element_type=F32).astype(BF16)


def _proj_k_body(x_ref, g_ref, w_ref, gain_ref, gmat_ref, cos_ref, sina_ref, sinb_ref, o_ref):
    xn = _rms_rows(x_ref[...], g_ref[...]).astype(BF16)
    p = jnp.dot(xn, w_ref[...], preferred_element_type=F32)
    for h in range(N_HEADS):
        sl = slice(h * V_HEAD_DIM, (h + 1) * V_HEAD_DIM)
        t = _qk_norm_rope(p[:, sl], gain_ref[...], gmat_ref[...], cos_ref[...], sina_ref[...], sinb_ref[...])
        o_ref[:, sl] = t.astype(BF16)


def _proj_q_body(x_ref, g_ref, w_ref, gain_ref, gmat_ref, cos_ref, sina_ref, sinb_ref, o_ref, *, scale):
    xn = _rms_rows(x_ref[...], g_ref[...]).astype(BF16)
    p = jnp.dot(xn, w_ref[...], preferred_element_type=F32)
    for h in range(N_HEADS):
        sl = slice(h * V_HEAD_DIM, (h + 1) * V_HEAD_DIM)
        t = _qk_norm_rope(p[:, sl], gain_ref[...], gmat_ref[...], cos_ref[...], sina_ref[...], sinb_ref[...])
        o_ref[h] = (t * scale).T.astype(BF16)


def _proj_v_body(x_ref, g_ref, w_ref, o_ref):
    xn = _rms_rows(x_ref[...], g_ref[...]).astype(BF16)
    p = jnp.dot(xn, w_ref[...], preferred_element_type=F32)
    for h in range(N_HEADS):
        o_ref[h, 0] = p[:, h * V_HEAD_DIM:(h + 1) * V_HEAD_DIM].T.astype(BF16)


def _proj(mode, x, g, w_in, layer, col_block, extras=(), scale=None):
    s, d = x.shape
    n = N_HEADS * V_HEAD_DIM
    tm = min(ROW_TILE, s)
    in_specs = [
        pl.BlockSpec((tm, d), lambda i: (i, 0)),
        pl.BlockSpec((1, d), lambda i: (0, 0)),
        pl.BlockSpec((None, d, n), lambda i: (layer, 0, col_block)),
    ]
    if mode in ("q", "k"):
        gain, gmat, cos, sina, sinb = extras
        in_specs += [
            pl.BlockSpec((1, V_HEAD_DIM), lambda i: (0, 0)),
            pl.BlockSpec((V_HEAD_DIM, V_HEAD_DIM), lambda i: (0, 0)),
            pl.BlockSpec((tm, V_HEAD_DIM), lambda i: (i, 0)),
            pl.BlockSpec((tm, V_HEAD_DIM), lambda i: (i, 0)),
            pl.BlockSpec((tm, V_HEAD_DIM), lambda i: (i, 0)),
        ]
    if mode == "f":
        body, out_shape = _proj_f_body, jax.ShapeDtypeStruct((s, n), BF16)
        out_spec = pl.BlockSpec((tm, n), lambda i: (i, 0))
    elif mode == "k":
        body, out_shape = _proj_k_body, jax.ShapeDtypeStruct((s, n), BF16)
        out_spec = pl.BlockSpec((tm, n), lambda i: (i, 0))
    elif mode == "q":
        body = functools.partial(_proj_q_body, scale=scale)
        out_shape = jax.ShapeDtypeStruct((N_HEADS, V_HEAD_DIM, s), BF16)
        out_spec = pl.BlockSpec((N_HEADS, V_HEAD_DIM, tm), lambda i: (0, 0, i))
    else:
        assert mode == "v" and tm == KV_TILE
        body = _proj_v_body
        out_shape = jax.ShapeDtypeStruct((N_HEADS, s // tm, V_HEAD_DIM, tm), BF16)
        out_spec = pl.BlockSpec((N_HEADS, 1, V_HEAD_DIM, tm), lambda i: (0, i, 0, 0))
    return pl.pallas_call(
        body,
        grid=(s // tm,),
        in_specs=in_specs,
        out_specs=out_spec,
        out_shape=out_shape,
        compiler_params=_params("parallel"),
        name="proj_" + mode,
    )(x, g, w_in, *extras)


def _dft_tables(s):
    n1, n2 = s // DFT_S2, DFT_S2
    a1 = 2.0 * np.pi * np.outer(np.arange(n1), np.arange(n1)) / n1
    cs1 = np.concatenate([np.cos(a1), -np.sin(a1)], axis=0)
    at = 2.0 * np.pi * np.outer(np.arange(n1), np.arange(n2)) / s
    nt = n2 // F1_S2_TILE
    twc = np.cos(at).reshape(n1, nt, F1_S2_TILE).transpose(1, 0, 2)
    tws = np.sin(at).reshape(n1, nt, F1_S2_TILE).transpose(1, 0, 2)
    a2 = 2.0 * np.pi * np.outer(np.arange(n2), np.arange(n2)) / n2
    c2, s2 = np.cos(a2), np.sin(a2)
    f2 = np.block([[c2, s2], [-s2, c2]])
    ac = 2.0 * np.pi * np.outer(np.arange(FOURIER_GROUP_DIM), np.arange(FOURIER_GROUP_DIM)) / FOURIER_GROUP_DIM
    cc = np.concatenate([np.cos(ac), np.sin(ac)], axis=0)
    return (jnp.asarray(cs1, dtype=BF16), jnp.asarray(twc, dtype=F32), jnp.asarray(tws, dtype=F32),
            jnp.asarray(f2, dtype=BF16), jnp.asarray(cc, dtype=BF16))


def _f1_body(u_ref, cs_ref, twc_ref, tws_ref, z_ref, *, n1, ncol):
    zc = jnp.dot(cs_ref[...], u_ref[...], preferred_element_type=F32)
    twc = twc_ref[0]
    tws = tws_ref[0]
    for j in range(F1_S2_TILE):
        zr = zc[0:n1, j * ncol:(j + 1) * ncol]
        zi = zc[n1:2 * n1, j * ncol:(j + 1) * ncol]
        c = twc[:, j:j + 1]
        s = tws[:, j:j + 1]
        z_ref[0, j] = (zr * c + zi * s).astype(BF16)
        z_ref[1, j] = (zi * c - zr * s).astype(BF16)


def _f2_body(z_ref, f_ref, cc_ref, y_ref, *, norm):
    n2 = DFT_S2
    gd = FOURIER_GROUP_DIM
    zz = z_ref[...].reshape(2 * n2, z_ref.shape[-1])
    xb = jnp.dot(f_ref[...], zz, preferred_element_type=F32).astype(BF16)
    for g in range(N_FOURIER_GROUPS):
        lhs = jnp.concatenate([xb[0:n2, g * gd:(g + 1) * gd], xb[n2:2 * n2, g * gd:(g + 1) * gd]], axis=1)
        y = jnp.dot(lhs, cc_ref[...], preferred_element_type=F32)
        y_ref[:, g * gd:(g + 1) * gd] = (y * norm).astype(BF16)


def _fourier_mix(u, tables):
    s, ncol = u.shape
    n1, n2 = s // DFT_S2, DFT_S2
    cs1, twc, tws, f2, cc = tables
    ts2 = F1_S2_TILE
    z = pl.pallas_call(
        functools.partial(_f1_body, n1=n1, ncol=ncol),
        grid=(n2 // ts2,),
        in_specs=[
            pl.BlockSpec((n1, ts2 * ncol), lambda t: (0, t)),
            pl.BlockSpec((2 * n1, n1), lambda t: (0, 0)),
            pl.BlockSpec((1, n1, ts2), lambda t: (t, 0, 0)),
            pl.BlockSpec((1, n1, ts2), lambda t: (t, 0, 0)),
        ],
        out_specs=pl.BlockSpec((2, ts2, n1, ncol), lambda t: (0, t, 0, 0)),
        out_shape=jax.ShapeDtypeStruct((2, n2, n1, ncol), BF16),
        compiler_params=_params("parallel"),
        name="dft_stage1",
    )(u.reshape(n1, n2 * ncol), cs1, twc, tws)
    norm = 1.0 / math.sqrt(s * FOURIER_GROUP_DIM)
    y = pl.pallas_call(
        functools.partial(_f2_body, norm=norm),
        grid=(n1,),
        in_specs=[
            pl.BlockSpec((2, n2, ncol), lambda k: (0, 0, k)),
            pl.BlockSpec((2 * n2, 2 * n2), lambda k: (0, 0)),
            pl.BlockSpec((2 * FOURIER_GROUP_DIM, FOURIER_GROUP_DIM), lambda k: (0, 0)),
        ],
        out_specs=pl.BlockSpec((n2, ncol), lambda k: (0, k)),
        out_shape=jax.ShapeDtypeStruct((n2, n1 * ncol), BF16),
        compiler_params=_params("parallel"),
        name="dft_stage2",
    )(z.reshape(2, n2, n1 * ncol), f2, cc)
    return y.reshape(s, ncol)


def _score_bound(gq, gk):
    bq = jnp.max(jnp.abs(gq), axis=-1, keepdims=True)
    bk = jnp.max(jnp.abs(gk), axis=-1, keepdims=True)
    return (QK_HEAD_DIM * (QK_HEAD_DIM ** -0.5) * LOG2_E) * bq * bk


def _attn_prologue(qt_ref):
    qt = qt_ref[0]
    row = lax.broadcasted_iota(jnp.int32, qt.shape, 0)
    zero = jnp.zeros_like(qt)
    return jnp.where(row < QK_HEAD_DIM, qt, zero), jnp.where(row >= QK_HEAD_DIM, qt, zero)


def _attn_epilogue(o1, o2, lq1_ref, lk1_ref, lq2_ref, lk2_ref, gsub_ref, o_ref, lambda_init):
    lam = (jnp.exp(jnp.sum(lq1_ref[...] * lk1_ref[...], axis=-1, keepdims=True))
           - jnp.exp(jnp.sum(lq2_ref[...] * lk2_ref[...], axis=-1, keepdims=True)) + lambda_init)
    o = o1 - lam * o2
    ms = jnp.mean(o * o, axis=0, keepdims=True)
    y = o * lax.rsqrt(ms + SUBLN_EPS) * gsub_ref[...] * (1.0 - lambda_init)
    o_ref[...] = y.T.astype(BF16)


def _attn_online_body(qt_ref, k_ref, vt_ref, gq_ref, gk_ref, lq1_ref, lk1_ref, lq2_ref, lk2_ref, gsub_ref, o_ref,
                      m_ref, l_ref, acc_ref, *, lambda_init):
    del gq_ref, gk_ref
    nk, tk = vt_ref.shape[1], vt_ref.shape[3]
    q_halves = _attn_prologue(qt_ref)
    m_ref[...] = jnp.full(m_ref.shape, M_INIT, F32)
    l_ref[...] = jnp.zeros(l_ref.shape, F32)
    acc_ref[...] = jnp.zeros(acc_ref.shape, F32)

    def chunk(c, carry):
        off = pl.multiple_of(c * tk, tk)
        kc = k_ref[pl.ds(off, tk), :]
        vc = vt_ref[0, c]
        for hf in range(2):
            s = jnp.dot(kc, q_halves[hf], preferred_element_type=F32)
            m_old = m_ref[hf]
            m_new = jnp.maximum(m_old, jnp.max(s, axis=0, keepdims=True))
            alpha = jnp.exp2(m_old - m_new)
            p = jnp.exp2(s - m_new)
            l_ref[hf] = alpha * l_ref[hf] + jnp.sum(p, axis=0, keepdims=True)
            acc_ref[hf] = alpha * acc_ref[hf] + jnp.dot(vc, p.astype(BF16), preferred_element_type=F32)
            m_ref[hf] = m_new
        return carry

    lax.fori_loop(0, nk, chunk, 0)
    _attn_epilogue(acc_ref[0] / l_ref[0], acc_ref[1] / l_ref[1],
                   lq1_ref, lk1_ref, lq2_ref, lk2_ref, gsub_ref, o_ref, lambda_init)


def _attn_bounded_body(qt_ref, k_ref, vt_ref, gq_ref, gk_ref, lq1_ref, lk1_ref, lq2_ref, lk2_ref, gsub_ref, o_ref,
                       l_ref, acc_ref, *, lambda_init):
    nk, tk = vt_ref.shape[1], vt_ref.shape[3]
    tq = qt_ref.shape[2]
    q_halves = _attn_prologue(qt_ref)
    bound = _score_bound(gq_ref[...], gk_ref[...])
    l_ref[...] = jnp.zeros(l_ref.shape, F32)
    acc_ref[...] = jnp.zeros(acc_ref.shape, F32)

    per_chunk = tk // KV_UNIT
    unroll = math.gcd(nk, KV_UNROLL)
    units = [(cc, j, hf) for cc in range(unroll) for j in range(per_chunk) for hf in range(2)]

    def trip(t, carry):
        def scores(cc, j, hf):
            off = pl.multiple_of((t * unroll + cc) * tk + j * KV_UNIT, KV_UNIT)
            return jnp.dot(k_ref[pl.ds(off, KV_UNIT), :], q_halves[hf], preferred_element_type=F32)

        pending = [scores(*u) for u in units[:QK_LOOKAHEAD]]
        for n, (cc, j, hf) in enumerate(units):
            s = pending.pop(0)
            if n + QK_LOOKAHEAD < len(units):
                pending.append(scores(*units[n + QK_LOOKAHEAD]))
            p = jnp.exp2(s - bound)
            l_ref[hf] += jnp.sum(p.reshape(KV_UNIT // 8, 8, tq), axis=0)
            vc = vt_ref[0, t * unroll + cc, :, j * KV_UNIT:(j + 1) * KV_UNIT]
            acc_ref[hf] += jnp.dot(vc, p.astype(BF16), preferred_element_type=F32)
        return carry

    lax.fori_loop(0, nk // unroll, trip, 0)
    l1 = jnp.sum(l_ref[0], axis=0, keepdims=True)
    l2 = jnp.sum(l_ref[1], axis=0, keepdims=True)
    _attn_epilogue(acc_ref[0] / l1, acc_ref[1] / l2,
                   lq1_ref, lk1_ref, lq2_ref, lk2_ref, gsub_ref, o_ref, lambda_init)


def _diff_attention(qt, k, vt, gq, gk, lq1, lk1, lq2, lk2, gsub, lambda_init, online):
    h, _, s = qt.shape
    nk, tk = vt.shape[1], vt.shape[3]
    tq = min(Q_TILE, s)
    vec = pl.BlockSpec((1, QK_HEAD_DIM), lambda hh, i: (0, 0))
    if online:
        body = _attn_online_body
        scratch = [pltpu.VMEM((2, 1, tq), F32), pltpu.VMEM((2, 1, tq), F32), pltpu.VMEM((2, V_HEAD_DIM, tq), F32)]
    else:
        body = _attn_bounded_body
        scratch = [pltpu.VMEM((2, 8, tq), F32), pltpu.VMEM((2, V_HEAD_DIM, tq), F32)]
    return pl.pallas_call(
        functools.partial(body, lambda_init=lambda_init),
        grid=(h, s // tq),
        in_specs=[
            pl.BlockSpec((1, V_HEAD_DIM, tq), lambda hh, i: (hh, 0, i)),
            pl.BlockSpec((s, V_HEAD_DIM), lambda hh, i: (0, hh)),
            pl.BlockSpec((1, nk, V_HEAD_DIM, tk), lambda hh, i: (hh, 0, 0, 0)),
            vec, vec, vec, vec, vec, vec,
            pl.BlockSpec((V_HEAD_DIM, 1), lambda hh, i: (0, 0)),
        ],
        out_specs=pl.BlockSpec((tq, V_HEAD_DIM), lambda hh, i: (i, hh)),
        out_shape=jax.ShapeDtypeStruct((s, h * V_HEAD_DIM), BF16),
        scratch_shapes=scratch,
        compiler_params=_params("parallel", "parallel"),
        name="diff_attention_online" if online else "diff_attention_bounded",
    )(qt, k, vt, gq, gk, lq1, lk1, lq2, lk2, gsub)


def _outproj_body(x_ref, yf_ref, ya_ref, wf_ref, wa_ref, o_ref):
    o_ref[...] = (x_ref[...]
                  + jnp.dot(yf_ref[...], wf_ref[...], preferred_element_type=F32)
                  + jnp.dot(ya_ref[...], wa_ref[...], preferred_element_type=F32))


def _outproj(x, yf, ya, w_out, layer):
    s, d = x.shape
    nf, na = yf.shape[1], ya.shape[1]
    assert nf == na
    tm = min(ROW_TILE, s)
    return pl.pallas_call(
        _outproj_body,
        grid=(s // tm,),
        in_specs=[
            pl.BlockSpec((tm, d), lambda i: (i, 0)),
            pl.BlockSpec((tm, nf), lambda i: (i, 0)),
            pl.BlockSpec((tm, na), lambda i: (i, 0)),
            pl.BlockSpec((None, nf, d), lambda i: (layer, 0, 0)),
            pl.BlockSpec((None, na, d), lambda i: (layer, 1, 0)),
        ],
        out_specs=pl.BlockSpec((tm, d), lambda i: (i, 0)),
        out_shape=jax.ShapeDtypeStruct((s, d), F32),
        compiler_params=_params("parallel"),
        name="out_proj",
    )(x, yf, ya, w_out, w_out)


def kernel(x, positions, norm_ffn1, w1_gate, w1_up, w1_down, norm_mix, w_in, q_norm, k_norm,
           lambda_q1, lambda_k1, lambda_q2, lambda_k2, subln, w_out, norm_ffn2, w2_gate, w2_up, w2_down):
    b, s, d = x.shape
    depth = w_in.shape[0]
    assert b == 1 and s % DFT_S2 == 0
    xs = x.reshape(s, d)
    win, wout = w_in.astype(BF16), w_out.astype(BF16)

    cos, sina, sinb = _rope_tables(positions.reshape(s))
    gmat = _group_mean_matrix()
    tables = _dft_tables(s)
    q_scale = (QK_HEAD_DIM ** -0.5) * LOG2_E

    for l in range(depth):
        lambda_init = 0.8 - 0.6 * math.exp(-0.3 * l)
        xs = _ffn(xs, norm_ffn1[l].reshape(1, d), w1_gate, w1_up, w1_down, l)

        g_mix = norm_mix[l].reshape(1, d)
        gq = jnp.tile(q_norm[l], 2).reshape(1, V_HEAD_DIM)
        gk = jnp.tile(k_norm[l], 2).reshape(1, V_HEAD_DIM)
        u_f = _proj("f", xs, g_mix, win, l, 0)
        qt = _proj("q", xs, g_mix, win, l, 1, (gq, gmat, cos, sina, sinb), scale=q_scale)
        kk = _proj("k", xs, g_mix, win, l, 2, (gk, gmat, cos, sina, sinb))
        vt = _proj("v", xs, g_mix, win, l, 3)
        y_f = _fourier_mix(u_f, tables)
        gq64, gk64 = q_norm[l].reshape(1, QK_HEAD_DIM), k_norm[l].reshape(1, QK_HEAD_DIM)
        attn_args = (qt, kk, vt, gq64, gk64,
                     lambda_q1[l].reshape(1, -1), lambda_k1[l].reshape(1, -1),
                     lambda_q2[l].reshape(1, -1), lambda_k2[l].reshape(1, -1),
                     subln[l].reshape(V_HEAD_DIM, 1))
        y_a = lax.cond(
            _score_bound(gq64, gk64)[0, 0] <= MAX_BOUNDED_SOFTMAX_LOG2,
            lambda *a: _diff_attention(*a, lambda_init, False),
            lambda *a: _diff_attention(*a, lambda_init, True),
            *attn_args)
        xs = _outproj(xs, y_f, y_a, wout, l)

        xs = _ffn(xs, norm_ffn2[l].reshape(1, d), w2_gate, w2_up, w2_down, l)
    return xs.reshape(b, s, d)
```

```python
import functools
import math

import numpy as np
import jax
import jax.numpy as jnp
from jax import lax
from jax.experimental import pallas as pl
from jax.experimental.pallas import tpu as pltpu

F32 = jnp.float32
BF16 = jnp.bfloat16

N_HEADS = 8
V_HEAD_DIM = 128
QK_HEAD_DIM = 64
N_FOURIER_GROUPS = 8
FOURIER_GROUP_DIM = 128
ROPE_THETA = 10000.0
NORM_EPS = 1e-6
SUBLN_EPS = 1e-5
LOG2_E = 1.4426950408889634

V7X_VMEM_BYTES = 64 * 1024 * 1024
VMEM_LIMIT_BYTES = V7X_VMEM_BYTES - 8 * 1024 * 1024
LANES = 128

ROW_TILE = 512
FFN_ROW_TILE = 1024
FF_TILE = 256
Q_TILE = 512
Q_TILES_PER_STEP = 4
KV_TILE = 512
DFT_S2 = 128
F1_S2_TILE = 8
F2_K1_PER_STEP = 4
KV_UNROLL = 16
KV_UNIT = 256
QK_LOOKAHEAD = 2
M_INIT = -1e30
MAX_BOUNDED_SOFTMAX_LOG2 = 60.0


def _params(*semantics):
    return pltpu.CompilerParams(dimension_semantics=semantics, vmem_limit_bytes=VMEM_LIMIT_BYTES)


def _rms_rows(x, g):
    ms = jnp.mean(x * x, axis=-1, keepdims=True)
    return x * lax.rsqrt(ms + NORM_EPS) * g


def _ffn_body(x_ref, g_ref, wg_ref, wu_ref, wd_ref, o_ref, xn_ref):
    @pl.when(pl.program_id(1) == 0)
    def _():
        x = x_ref[...]
        xn_ref[...] = _rms_rows(x, g_ref[...]).astype(BF16)
        o_ref[...] = x

    xn = xn_ref[...]
    h = jnp.dot(xn, wg_ref[...].astype(BF16), preferred_element_type=F32)
    u = jnp.dot(xn, wu_ref[...].astype(BF16), preferred_element_type=F32)
    a = (h * jax.nn.sigmoid(h) * u).astype(BF16)
    o_ref[...] += 0.5 * jnp.dot(a, wd_ref[...].astype(BF16), preferred_element_type=F32)


def _ffn(x, g, wg, wu, wd, layer):
    s, d = x.shape
    f = wg.shape[-1]
    tm, tf = min(FFN_ROW_TILE, s), min(FF_TILE, f)
    return pl.pallas_call(
        _ffn_body,
        grid=(s // tm, f // tf),
        in_specs=[
            pl.BlockSpec((tm, d), lambda i, j: (i, 0)),
            pl.BlockSpec((1, d), lambda i, j: (0, 0)),
            pl.BlockSpec((None, d, tf), lambda i, j: (layer, 0, j)),
            pl.BlockSpec((None, d, tf), lambda i, j: (layer, 0, j)),
            pl.BlockSpec((None, tf, d), lambda i, j: (layer, j, 0)),
        ],
        out_specs=pl.BlockSpec((tm, d), lambda i, j: (i, 0)),
        out_shape=jax.ShapeDtypeStruct((s, d), F32),
        scratch_shapes=[pltpu.VMEM((tm, d), BF16)],
        compiler_params=_params("parallel", "arbitrary"),
        name="ffn",
    )(x, g, wg, wu, wd)


def _rope_body(pos_ref, invf_ref, cos_ref, sina_ref, sinb_ref):
    ang = pos_ref[...] * invf_ref[...]
    c = jnp.cos(ang)
    s = jnp.sin(ang)
    lane = lax.broadcasted_iota(jnp.int32, ang.shape, 1)
    first = (lane & (QK_HEAD_DIM // 2)) == 0
    cos_ref[...] = c
    sina_ref[...] = jnp.where(first, -s, 0.0)
    sinb_ref[...] = jnp.where(first, 0.0, s)


def _rope_tables(positions):
    s = positions.shape[0]
    inv_freq = 1.0 / (ROPE_THETA ** (jnp.arange(0, QK_HEAD_DIM, 2, dtype=F32) / QK_HEAD_DIM))
    invf = jnp.tile(inv_freq, 2 * V_HEAD_DIM // QK_HEAD_DIM).reshape(1, V_HEAD_DIM)
    pos = positions.astype(F32).reshape(s, 1)
    ts = min(1024, s)
    tab = jax.ShapeDtypeStruct((s, V_HEAD_DIM), F32)
    return pl.pallas_call(
        _rope_body,
        grid=(s // ts,),
        in_specs=[pl.BlockSpec((ts, 1), lambda i: (i, 0)), pl.BlockSpec((1, V_HEAD_DIM), lambda i: (0, 0))],
        out_specs=[pl.BlockSpec((ts, V_HEAD_DIM), lambda i: (i, 0))] * 3,
        out_shape=[tab, tab, tab],
        compiler_params=_params("parallel"),
        name="rope_tables",
    )(pos, invf)


def _group_mean_matrix():
    idx = np.arange(V_HEAD_DIM) // QK_HEAD_DIM
    return jnp.asarray((idx[:, None] == idx[None, :]).astype(np.float32) / QK_HEAD_DIM, dtype=BF16)


def _qk_norm_rope(t, gain, gmat, cos, sina, sinb):
    sq = t * t
    hi = sq.astype(BF16)
    lo = (sq - hi.astype(F32)).astype(BF16)
    ms = jnp.dot(hi, gmat, preferred_element_type=F32) + jnp.dot(lo, gmat, preferred_element_type=F32)
    tn = t * lax.rsqrt(ms + NORM_EPS) * gain
    half = QK_HEAD_DIM // 2
    return tn * cos + pltpu.roll(tn, V_HEAD_DIM - half, 1) * sina + pltpu.roll(tn, half, 1) * sinb


def _in_proj_body(x_ref, g_ref, w_ref, gq_ref, gk_ref, gmat_ref, cos_ref, sina_ref, sinb_ref,
                  u_ref, qt_ref, k_ref, vt_ref, *, q_scale):
    n = N_HEADS * V_HEAD_DIM
    xn = _rms_rows(x_ref[...], g_ref[...]).astype(BF16)

    def group(c):
        return jnp.dot(xn, w_ref[:, c * n:(c + 1) * n], preferred_element_type=F32)

    u_ref[...] = group(0).astype(BF16)
    gmat, cos, sina, sinb = gmat_ref[...], cos_ref[...], sina_ref[...], sinb_ref[...]
    pq = group(1)
    for h in range(N_HEADS):
        t = _qk_norm_rope(pq[:, h * V_HEAD_DIM:(h + 1) * V_HEAD_DIM], gq_ref[...], gmat, cos, sina, sinb)
        qt_ref[h] = (t * q_scale).T.astype(BF16)
    pk = group(2)
    for h in range(N_HEADS):
        sl = slice(h * V_HEAD_DIM, (h + 1) * V_HEAD_DIM)
        k_ref[:, sl] = _qk_norm_rope(pk[:, sl], gk_ref[...], gmat, cos, sina, sinb).astype(BF16)
    pv = group(3)
    for h in range(N_HEADS):
        vt_ref[h, 0] = pv[:, h * V_HEAD_DIM:(h + 1) * V_HEAD_DIM].T.astype(BF16)


def _in_proj(x, g, w_in, layer, gq, gk, gmat, cos, sina, sinb, q_scale):
    s, d = x.shape
    n = N_HEADS * V_HEAD_DIM
    tm = min(ROW_TILE, s)
    assert tm == min(KV_TILE, s) and w_in.shape[-1] == 4 * n
    row_tab = pl.BlockSpec((tm, V_HEAD_DIM), lambda i: (i, 0))
    head_vec = pl.BlockSpec((1, V_HEAD_DIM), lambda i: (0, 0))
    return pl.pallas_call(
        functools.partial(_in_proj_body, q_scale=q_scale),
        grid=(s // tm,),
        in_specs=[
            pl.BlockSpec((tm, d), lambda i: (i, 0)),
            pl.BlockSpec((1, d), lambda i: (0, 0)),
            pl.BlockSpec((None, d, 4 * n), lambda i: (layer, 0, 0)),
            head_vec, head_vec,
            pl.BlockSpec((V_HEAD_DIM, V_HEAD_DIM), lambda i: (0, 0)),
            row_tab, row_tab, row_tab,
        ],
        out_specs=[
            pl.BlockSpec((tm, n), lambda i: (i, 0)),
            pl.BlockSpec((N_HEADS, V_HEAD_DIM, tm), lambda i: (0, 0, i)),
            pl.BlockSpec((tm, n), lambda i: (i, 0)),
            pl.BlockSpec((N_HEADS, 1, V_HEAD_DIM, tm), lambda i: (0, i, 0, 0)),
        ],
        out_shape=[
            jax.ShapeDtypeStruct((s, n), BF16),
            jax.ShapeDtypeStruct((N_HEADS, V_HEAD_DIM, s), BF16),
            jax.ShapeDtypeStruct((s, n), BF16),
            jax.ShapeDtypeStruct((N_HEADS, s // tm, V_HEAD_DIM, tm), BF16),
        ],
        compiler_params=_params("parallel"),
        name="in_proj",
    )(x, g, w_in, gq, gk, gmat, cos, sina, sinb)


def _dft_tables(s):
    n1, n2 = s // DFT_S2, DFT_S2
    a1 = 2.0 * np.pi * np.outer(np.arange(n1), np.arange(n1)) / n1
    cs1 = np.concatenate([np.cos(a1), -np.sin(a1)], axis=0)
    at = 2.0 * np.pi * np.outer(np.arange(n1), np.arange(n2)) / s
    nt = n2 // F1_S2_TILE
    twc = np.cos(at).reshape(n1, nt, F1_S2_TILE).transpose(1, 0, 2)
    tws = np.sin(at).reshape(n1, nt, F1_S2_TILE).transpose(1, 0, 2)
    a2 = 2.0 * np.pi * np.outer(np.arange(n2), np.arange(n2)) / n2
    c2, s2 = np.cos(a2), np.sin(a2)
    f2 = np.block([[c2, s2], [-s2, c2]])
    ac = 2.0 * np.pi * np.outer(np.arange(FOURIER_GROUP_DIM), np.arange(FOURIER_GROUP_DIM)) / FOURIER_GROUP_DIM
    cc = np.concatenate([np.cos(ac), np.sin(ac)], axis=0)
    return (jnp.asarray(cs1, dtype=BF16), jnp.asarray(twc, dtype=F32), jnp.asarray(tws, dtype=F32),
            jnp.asarray(f2, dtype=BF16), jnp.asarray(cc, dtype=BF16))


def _f1_body(u_ref, cs_ref, twc_ref, tws_ref, z_ref, *, n1, ncol):
    zc = jnp.dot(cs_ref[...], u_ref[...], preferred_element_type=F32)
    twc = twc_ref[0]
    tws = tws_ref[0]
    for j in range(F1_S2_TILE):
        zr = zc[0:n1, j * ncol:(j + 1) * ncol]
        zi = zc[n1:2 * n1, j * ncol:(j + 1) * ncol]
        c = twc[:, j:j + 1]
        s = tws[:, j:j + 1]
        z_ref[0, j] = (zr * c + zi * s).astype(BF16)
        z_ref[1, j] = (zi * c - zr * s).astype(BF16)


def _f2_body(z_ref, f_ref, cc_ref, y_ref, *, norm):
    n2 = DFT_S2
    gd = FOURIER_GROUP_DIM
    zz = z_ref[...].reshape(2 * n2, z_ref.shape[-1])
    xb = jnp.dot(f_ref[...], zz, preferred_element_type=F32).astype(BF16)
    for g in range(z_ref.shape[-1] // gd):
        lhs = jnp.concatenate([xb[0:n2, g * gd:(g + 1) * gd], xb[n2:2 * n2, g * gd:(g + 1) * gd]], axis=1)
        y = jnp.dot(lhs, cc_ref[...], preferred_element_type=F32)
        y_ref[:, g * gd:(g + 1) * gd] = (y * norm).astype(BF16)


def _fourier_mix(u, tables):
    s, ncol = u.shape
    n1, n2 = s // DFT_S2, DFT_S2
    cs1, twc, tws, f2, cc = tables
    ts2 = F1_S2_TILE
    z = pl.pallas_call(
        functools.partial(_f1_body, n1=n1, ncol=ncol),
        grid=(n2 // ts2,),
        in_specs=[
            pl.BlockSpec((n1, ts2 * ncol), lambda t: (0, t)),
            pl.BlockSpec((2 * n1, n1), lambda t: (0, 0)),
            pl.BlockSpec((1, n1, ts2), lambda t: (t, 0, 0)),
            pl.BlockSpec((1, n1, ts2), lambda t: (t, 0, 0)),
        ],
        out_specs=pl.BlockSpec((2, ts2, n1, ncol), lambda t: (0, t, 0, 0)),
        out_shape=jax.ShapeDtypeStruct((2, n2, n1, ncol), BF16),
        compiler_params=_params("parallel"),
        name="dft_stage1",
    )(u.reshape(n1, n2 * ncol), cs1, twc, tws)
    norm = 1.0 / math.sqrt(s * FOURIER_GROUP_DIM)
    kb = math.gcd(n1, F2_K1_PER_STEP)
    y = pl.pallas_call(
        functools.partial(_f2_body, norm=norm),
        grid=(n1 // kb,),
        in_specs=[
            pl.BlockSpec((2, n2, kb * ncol), lambda k: (0, 0, k)),
            pl.BlockSpec((2 * n2, 2 * n2), lambda k: (0, 0)),
            pl.BlockSpec((2 * FOURIER_GROUP_DIM, FOURIER_GROUP_DIM), lambda k: (0, 0)),
        ],
        out_specs=pl.BlockSpec((n2, kb * ncol), lambda k: (0, k)),
        out_shape=jax.ShapeDtypeStruct((n2, n1 * ncol), BF16),
        compiler_params=_params("parallel"),
        name="dft_stage2",
    )(z.reshape(2, n2, n1 * ncol), f2, cc)
    return y.reshape(s, ncol)


def _score_bound(gq, gk):
    bq = jnp.max(jnp.abs(gq), axis=-1, keepdims=True)
    bk = jnp.max(jnp.abs(gk), axis=-1, keepdims=True)
    return (QK_HEAD_DIM * (QK_HEAD_DIM ** -0.5) * LOG2_E) * bq * bk


def _for_each_q_tile(qt_ref, o_ref, tile_fn):
    tq = min(Q_TILE, qt_ref.shape[2])

    def sub(i, carry):
        off = pl.multiple_of(i * tq, tq)
        tile_fn(qt_ref[0, :, pl.ds(off, tq)], o_ref.at[pl.ds(off, tq), :])
        return carry

    lax.fori_loop(0, qt_ref.shape[2] // tq, sub, 0)


def _attn_prologue(qt):
    row = lax.broadcasted_iota(jnp.int32, qt.shape, 0)
    zero = jnp.zeros_like(qt)
    return jnp.where(row < QK_HEAD_DIM, qt, zero), jnp.where(row >= QK_HEAD_DIM, qt, zero)


def _attn_epilogue(o1, o2, lq1_ref, lk1_ref, lq2_ref, lk2_ref, gsub_ref, o_ref, lambda_init):
    lam = (jnp.exp(jnp.sum(lq1_ref[...] * lk1_ref[...], axis=-1, keepdims=True))
           - jnp.exp(jnp.sum(lq2_ref[...] * lk2_ref[...], axis=-1, keepdims=True)) + lambda_init)
    o = o1 - lam * o2
    ms = jnp.mean(o * o, axis=0, keepdims=True)
    y = o * lax.rsqrt(ms + SUBLN_EPS) * gsub_ref[...] * (1.0 - lambda_init)
    o_ref[...] = y.T.astype(BF16)


def _attn_online_body(qt_ref, k_ref, vt_ref, gq_ref, gk_ref, lq1_ref, lk1_ref, lq2_ref, lk2_ref, gsub_ref, o_ref,
                      m_ref, l_ref, acc_ref, *, lambda_init):
    del gq_ref, gk_ref
    nk, tk = vt_ref.shape[1], vt_ref.shape[3]

    def tile(qt, o_view):
        q_halves = _attn_prologue(qt)
        m_ref[...] = jnp.full(m_ref.shape, M_INIT, F32)
        l_ref[...] = jnp.zeros(l_ref.shape, F32)
        acc_ref[...] = jnp.zeros(acc_ref.shape, F32)

        def chunk(c, carry):
            off = pl.multiple_of(c * tk, tk)
            kc = k_ref[pl.ds(off, tk), :]
            vc = vt_ref[0, c]
            for hf in range(2):
                s = jnp.dot(kc, q_halves[hf], preferred_element_type=F32)
                m_old = m_ref[hf]
                m_new = jnp.maximum(m_old, jnp.max(s, axis=0, keepdims=True))
                alpha = jnp.exp2(m_old - m_new)
                p = jnp.exp2(s - m_new)
                l_ref[hf] = alpha * l_ref[hf] + jnp.sum(p, axis=0, keepdims=True)
                acc_ref[hf] = alpha * acc_ref[hf] + jnp.dot(vc, p.astype(BF16), preferred_element_type=F32)
                m_ref[hf] = m_new
            return carry

        lax.fori_loop(0, nk, chunk, 0)
        _attn_epilogue(acc_ref[0] / l_ref[0], acc_ref[1] / l_ref[1],
                       lq1_ref, lk1_ref, lq2_ref, lk2_ref, gsub_ref, o_view, lambda_init)

    _for_each_q_tile(qt_ref, o_ref, tile)


def _attn_bounded_body(qt_ref, k_ref, vt_ref, gq_ref, gk_ref, lq1_ref, lk1_ref, lq2_ref, lk2_ref, gsub_ref, o_ref,
                       l_ref, acc_ref, *, lambda_init):
    nk, tk = vt_ref.shape[1], vt_ref.shape[3]
    bound = _score_bound(gq_ref[...], gk_ref[...])

    per_chunk = tk // KV_UNIT
    unroll = math.gcd(nk, KV_UNROLL)
    units = [(cc, j, hf) for cc in range(unroll) for j in range(per_chunk) for hf in range(2)]

    def tile(qt, o_view):
        tq = qt.shape[1]
        q_halves = _attn_prologue(qt)
        l_ref[...] = jnp.zeros(l_ref.shape, F32)
        acc_ref[...] = jnp.zeros(acc_ref.shape, F32)

        def trip(t, carry):
            def scores(cc, j, hf):
                off = pl.multiple_of((t * unroll + cc) * tk + j * KV_UNIT, KV_UNIT)
                return jnp.dot(k_ref[pl.ds(off, KV_UNIT), :], q_halves[hf], preferred_element_type=F32)

            pending = [scores(*u) for u in units[:QK_LOOKAHEAD]]
            for n, (cc, j, hf) in enumerate(units):
                s = pending.pop(0)
                if n + QK_LOOKAHEAD < len(units):
                    pending.append(scores(*units[n + QK_LOOKAHEAD]))
                p = jnp.exp2(s - bound)
                l_ref[hf] += jnp.sum(p.reshape(KV_UNIT // 8, 8, tq), axis=0)
                vc = vt_ref[0, t * unroll + cc, :, j * KV_UNIT:(j + 1) * KV_UNIT]
                acc_ref[hf] += jnp.dot(vc, p.astype(BF16), preferred_element_type=F32)
            return carry

        lax.fori_loop(0, nk // unroll, trip, 0)
        l1 = jnp.sum(l_ref[0], axis=0, keepdims=True)
        l2 = jnp.sum(l_ref[1], axis=0, keepdims=True)
        _attn_epilogue(acc_ref[0] / l1, acc_ref[1] / l2,
                       lq1_ref, lk1_ref, lq2_ref, lk2_ref, gsub_ref, o_view, lambda_init)

    _for_each_q_tile(qt_ref, o_ref, tile)


def _diff_attention(qt, k, vt, gq, gk, lq1, lk1, lq2, lk2, gsub, lambda_init, online):
    h, _, s = qt.shape
    nk, tk = vt.shape[1], vt.shape[3]
    tq = min(Q_TILE, s)
    tqb = min(tq * Q_TILES_PER_STEP, s)
    vec = pl.BlockSpec((1, QK_HEAD_DIM), lambda hh, i: (0, 0))
    if online:
        body = _attn_online_body
        scratch = [pltpu.VMEM((2, 1, tq), F32), pltpu.VMEM((2, 1, tq), F32), pltpu.VMEM((2, V_HEAD_DIM, tq), F32)]
    else:
        body = _attn_bounded_body
        scratch = [pltpu.VMEM((2, 8, tq), F32), pltpu.VMEM((2, V_HEAD_DIM, tq), F32)]
    return pl.pallas_call(
        functools.partial(body, lambda_init=lambda_init),
        grid=(h, s // tqb),
        in_specs=[
            pl.BlockSpec((1, V_HEAD_DIM, tqb), lambda hh, i: (hh, 0, i)),
            pl.BlockSpec((s, V_HEAD_DIM), lambda hh, i: (0, hh)),
            pl.BlockSpec((1, nk, V_HEAD_DIM, tk), lambda hh, i: (hh, 0, 0, 0)),
            vec, vec, vec, vec, vec, vec,
            pl.BlockSpec((V_HEAD_DIM, 1), lambda hh, i: (0, 0)),
        ],
        out_specs=pl.BlockSpec((tqb, V_HEAD_DIM), lambda hh, i: (i, hh)),
        out_shape=jax.ShapeDtypeStruct((s, h * V_HEAD_DIM), BF16),
        scratch_shapes=scratch,
        compiler_params=_params("parallel", "parallel"),
        name="diff_attention_online" if online else "diff_attention_bounded",
    )(qt, k, vt, gq, gk, lq1, lk1, lq2, lk2, gsub)


def _outproj_body(x_ref, yf_ref, ya_ref, wf_ref, wa_ref, o_ref):
    o_ref[...] = (x_ref[...]
                  + jnp.dot(yf_ref[...], wf_ref[...], preferred_element_type=F32)
                  + jnp.dot(ya_ref[...], wa_ref[...], preferred_element_type=F32))


def _outproj(x, yf, ya, w_out, layer):
    s, d = x.shape
    nf, na = yf.shape[1], ya.shape[1]
    assert nf == na
    tm = min(ROW_TILE, s)
    return pl.pallas_call(
        _outproj_body,
        grid=(s // tm,),
        in_specs=[
            pl.BlockSpec((tm, d), lambda i: (i, 0)),
            pl.BlockSpec((tm, nf), lambda i: (i, 0)),
            pl.BlockSpec((tm, na), lambda i: (i, 0)),
            pl.BlockSpec((None, nf, d), lambda i: (layer, 0, 0)),
            pl.BlockSpec((None, na, d), lambda i: (layer, 1, 0)),
        ],
        out_specs=pl.BlockSpec((tm, d), lambda i: (i, 0)),
        out_shape=jax.ShapeDtypeStruct((s, d), F32),
        compiler_params=_params("parallel"),
        name="out_proj",
    )(x, yf, ya, w_out, w_out)


def kernel(x, positions, norm_ffn1, w1_gate, w1_up, w1_down, norm_mix, w_in, q_norm, k_norm,
           lambda_q1, lambda_k1, lambda_q2, lambda_k2, subln, w_out, norm_ffn2, w2_gate, w2_up, w2_down):
    b, s, d = x.shape
    depth = w_in.shape[0]
    assert b == 1 and s % DFT_S2 == 0
    xs = x.reshape(s, d)
    win, wout = w_in.astype(BF16), w_out.astype(BF16)

    cos, sina, sinb = _rope_tables(positions.reshape(s))
    gmat = _group_mean_matrix()
    tables = _dft_tables(s)
    q_scale = (QK_HEAD_DIM ** -0.5) * LOG2_E

    for l in range(depth):
        lambda_init = 0.8 - 0.6 * math.exp(-0.3 * l)
        xs = _ffn(xs, norm_ffn1[l].reshape(1, d), w1_gate, w1_up, w1_down, l)

        g_mix = norm_mix[l].reshape(1, d)
        gq = jnp.tile(q_norm[l], 2).reshape(1, V_HEAD_DIM)
        gk = jnp.tile(k_norm[l], 2).reshape(1, V_HEAD_DIM)
        u_f, qt, kk, vt = _in_proj(xs, g_mix, win, l, gq, gk, gmat, cos, sina, sinb, q_scale)
        y_f = _fourier_mix(u_f, tables)
        gq64, gk64 = q_norm[l].reshape(1, QK_HEAD_DIM), k_norm[l].reshape(1, QK_HEAD_DIM)
        attn_args = (qt, kk, vt, gq64, gk64,
                     lambda_q1[l].reshape(1, -1), lambda_k1[l].reshape(1, -1),
                     lambda_q2[l].reshape(1, -1), lambda_k2[l].reshape(1, -1),
                     subln[l].reshape(V_HEAD_DIM, 1))
        y_a = lax.cond(
            _score_bound(gq64, gk64)[0, 0] <= MAX_BOUNDED_SOFTMAX_LOG2,
            lambda *a: _diff_attention(*a, lambda_init, False),
            lambda *a: _diff_attention(*a, lambda_init, True),
            *attn_args)
        xs = _outproj(xs, y_f, y_a, wout, l)

        xs = _ffn(xs, norm_ffn2[l].reshape(1, d), w2_gate, w2_up, w2_down, l)
    return xs.reshape(b, s, d)
```

```python
import functools
import math

import numpy as np
import jax
import jax.numpy as jnp
from jax import lax
from jax.experimental import pallas as pl
from jax.experimental.pallas import tpu as pltpu

F32 = jnp.float32
BF16 = jnp.bfloat16

N_HEADS = 8
V_HEAD_DIM = 128
QK_HEAD_DIM = 64
N_FOURIER_GROUPS = 8
FOURIER_GROUP_DIM = 128
ROPE_THETA = 10000.0
NORM_EPS = 1e-6
SUBLN_EPS = 1e-5
LOG2_E = 1.4426950408889634

V7X_VMEM_BYTES = 64 * 1024 * 1024
VMEM_LIMIT_BYTES = V7X_VMEM_BYTES - 4 * 1024 * 1024
LANES = 128

ROW_TILE = 512
FFN_ROW_TILE = 1024
FF_TILE = 512
Q_TILE = 512
Q_TILES_PER_STEP = 4
KV_TILE = 512
DFT_S2 = 128
F1_S2_TILE = 8
F2_K1_PER_STEP = 4
KV_UNROLL = 16
KV_UNIT = 256
QK_LOOKAHEAD = 2
M_INIT = -1e30
MAX_BOUNDED_SOFTMAX_LOG2 = 60.0


def _params(*semantics):
    return pltpu.CompilerParams(dimension_semantics=semantics, vmem_limit_bytes=VMEM_LIMIT_BYTES)


def _rms_rows(x, g):
    ms = jnp.mean(x * x, axis=-1, keepdims=True)
    return x * lax.rsqrt(ms + NORM_EPS) * g


def _ffn_body(x_hbm, g_ref, wg_ref, wu_ref, wd_ref, o_ref, xrow_ref, xn_ref, x_sem):
    i, j = pl.program_id(0), pl.program_id(1)
    tm = o_ref.shape[0]

    def x_tile_copy(tile):
        return pltpu.make_async_copy(x_hbm.at[pl.ds(tile * tm, tm), :], xrow_ref, x_sem)

    @pl.when(jnp.logical_and(i == 0, j == 0))
    def _():
        x_tile_copy(0).start()

    @pl.when(j == 0)
    def _():
        x_tile_copy(i).wait()
        x = xrow_ref[...]
        xn_ref[...] = _rms_rows(x, g_ref[...]).astype(BF16)
        o_ref[...] = x

        @pl.when(i + 1 < pl.num_programs(0))
        def _():
            x_tile_copy(i + 1).start()

    xn = xn_ref[...]
    h = jnp.dot(xn, wg_ref[...].astype(BF16), preferred_element_type=F32)
    u = jnp.dot(xn, wu_ref[...].astype(BF16), preferred_element_type=F32)
    a = (h * jax.nn.sigmoid(h) * u).astype(BF16)
    o_ref[...] += 0.5 * jnp.dot(a, wd_ref[...].astype(BF16), preferred_element_type=F32)


def _ffn(x, g, wg, wu, wd, layer):
    s, d = x.shape
    f = wg.shape[-1]
    tm, tf = min(FFN_ROW_TILE, s), min(FF_TILE, f)
    return pl.pallas_call(
        _ffn_body,
        grid=(s // tm, f // tf),
        in_specs=[
            pl.BlockSpec(memory_space=pl.ANY),
            pl.BlockSpec((1, d), lambda i, j: (0, 0)),
            pl.BlockSpec((None, d, tf), lambda i, j: (layer, 0, j)),
            pl.BlockSpec((None, d, tf), lambda i, j: (layer, 0, j)),
            pl.BlockSpec((None, tf, d), lambda i, j: (layer, j, 0)),
        ],
        out_specs=pl.BlockSpec((tm, d), lambda i, j: (i, 0)),
        out_shape=jax.ShapeDtypeStruct((s, d), F32),
        scratch_shapes=[pltpu.VMEM((tm, d), F32), pltpu.VMEM((tm, d), BF16), pltpu.SemaphoreType.DMA(())],
        compiler_params=_params("arbitrary", "arbitrary"),
        name="ffn",
    )(x, g, wg, wu, wd)


def _rope_body(pos_ref, invf_ref, cos_ref, sina_ref, sinb_ref):
    ang = pos_ref[...] * invf_ref[...]
    c = jnp.cos(ang)
    s = jnp.sin(ang)
    lane = lax.broadcasted_iota(jnp.int32, ang.shape, 1)
    first = (lane & (QK_HEAD_DIM // 2)) == 0
    cos_ref[...] = c
    sina_ref[...] = jnp.where(first, -s, 0.0)
    sinb_ref[...] = jnp.where(first, 0.0, s)


def _rope_tables(positions):
    s = positions.shape[0]
    inv_freq = 1.0 / (ROPE_THETA ** (jnp.arange(0, QK_HEAD_DIM, 2, dtype=F32) / QK_HEAD_DIM))
    invf = jnp.tile(inv_freq, 2 * V_HEAD_DIM // QK_HEAD_DIM).reshape(1, V_HEAD_DIM)
    pos = positions.astype(F32).reshape(s, 1)
    ts = min(1024, s)
    tab = jax.ShapeDtypeStruct((s, V_HEAD_DIM), F32)
    return pl.pallas_call(
        _rope_body,
        grid=(s // ts,),
        in_specs=[pl.BlockSpec((ts, 1), lambda i: (i, 0)), pl.BlockSpec((1, V_HEAD_DIM), lambda i: (0, 0))],
        out_specs=[pl.BlockSpec((ts, V_HEAD_DIM), lambda i: (i, 0))] * 3,
        out_shape=[tab, tab, tab],
        compiler_params=_params("parallel"),
        name="rope_tables",
    )(pos, invf)


def _group_mean_matrix():
    idx = np.arange(V_HEAD_DIM) // QK_HEAD_DIM
    return jnp.asarray((idx[:, None] == idx[None, :]).astype(np.float32) / QK_HEAD_DIM, dtype=BF16)


def _qk_norm_rope(t, gain, gmat, cos, sina, sinb):
    sq = t * t
    hi = sq.astype(BF16)
    lo = (sq - hi.astype(F32)).astype(BF16)
    ms = jnp.dot(hi, gmat, preferred_element_type=F32) + jnp.dot(lo, gmat, preferred_element_type=F32)
    tn = t * lax.rsqrt(ms + NORM_EPS) * gain
    half = QK_HEAD_DIM // 2
    return tn * cos + pltpu.roll(tn, V_HEAD_DIM - half, 1) * sina + pltpu.roll(tn, half, 1) * sinb


def _in_proj_body(x_ref, g_ref, w_ref, gq_ref, gk_ref, gmat_ref, cos_ref, sina_ref, sinb_ref,
                  u_ref, qt_ref, k_ref, vt_ref, *, q_scale):
    n = N_HEADS * V_HEAD_DIM
    xn = _rms_rows(x_ref[...], g_ref[...]).astype(BF16)

    def group(c):
        return jnp.dot(xn, w_ref[:, c * n:(c + 1) * n], preferred_element_type=F32)

    u_ref[...] = group(0).astype(BF16)
    gmat, cos, sina, sinb = gmat_ref[...], cos_ref[...], sina_ref[...], sinb_ref[...]
    pq = group(1)
    for h in range(N_HEADS):
        t = _qk_norm_rope(pq[:, h * V_HEAD_DIM:(h + 1) * V_HEAD_DIM], gq_ref[...], gmat, cos, sina, sinb)
        qt_ref[h] = (t * q_scale).T.astype(BF16)
    pk = group(2)
    for h in range(N_HEADS):
        sl = slice(h * V_HEAD_DIM, (h + 1) * V_HEAD_DIM)
        k_ref[:, sl] = _qk_norm_rope(pk[:, sl], gk_ref[...], gmat, cos, sina, sinb).astype(BF16)
    pv = group(3)
    for h in range(N_HEADS):
        vt_ref[h, 0] = pv[:, h * V_HEAD_DIM:(h + 1) * V_HEAD_DIM].T.astype(BF16)


def _in_proj(x, g, w_in, layer, gq, gk, gmat, cos, sina, sinb, q_scale):
    s, d = x.shape
    n = N_HEADS * V_HEAD_DIM
    tm = min(ROW_TILE, s)
    assert tm == min(KV_TILE, s) and w_in.shape[-1] == 4 * n
    row_tab = pl.BlockSpec((tm, V_HEAD_DIM), lambda i: (i, 0))
    head_vec = pl.BlockSpec((1, V_HEAD_DIM), lambda i: (0, 0))
    return pl.pallas_call(
        functools.partial(_in_proj_body, q_scale=q_scale),
        grid=(s // tm,),
        in_specs=[
            pl.BlockSpec((tm, d), lambda i: (i, 0)),
            pl.BlockSpec((1, d), lambda i: (0, 0)),
            pl.BlockSpec((None, d, 4 * n), lambda i: (layer, 0, 0)),
            head_vec, head_vec,
            pl.BlockSpec((V_HEAD_DIM, V_HEAD_DIM), lambda i: (0, 0)),
            row_tab, row_tab, row_tab,
        ],
        out_specs=[
            pl.BlockSpec((tm, n), lambda i: (i, 0)),
            pl.BlockSpec((N_HEADS, V_HEAD_DIM, tm), lambda i: (0, 0, i)),
            pl.BlockSpec((tm, n), lambda i: (i, 0)),
            pl.BlockSpec((N_HEADS, 1, V_HEAD_DIM, tm), lambda i: (0, i, 0, 0)),
        ],
        out_shape=[
            jax.ShapeDtypeStruct((s, n), BF16),
            jax.ShapeDtypeStruct((N_HEADS, V_HEAD_DIM, s), BF16),
            jax.ShapeDtypeStruct((s, n), BF16),
            jax.ShapeDtypeStruct((N_HEADS, s // tm, V_HEAD_DIM, tm), BF16),
        ],
        compiler_params=_params("parallel"),
        name="in_proj",
    )(x, g, w_in, gq, gk, gmat, cos, sina, sinb)


def _dft_tables(s):
    n1, n2 = s // DFT_S2, DFT_S2
    a1 = 2.0 * np.pi * np.outer(np.arange(n1), np.arange(n1)) / n1
    cs1 = np.concatenate([np.cos(a1), -np.sin(a1)], axis=0)
    at = 2.0 * np.pi * np.outer(np.arange(n1), np.arange(n2)) / s
    nt = n2 // F1_S2_TILE
    twc = np.cos(at).reshape(n1, nt, F1_S2_TILE).transpose(1, 0, 2)
    tws = np.sin(at).reshape(n1, nt, F1_S2_TILE).transpose(1, 0, 2)
    a2 = 2.0 * np.pi * np.outer(np.arange(n2), np.arange(n2)) / n2
    c2, s2 = np.cos(a2), np.sin(a2)
    f2 = np.block([[c2, s2], [-s2, c2]])
    ac = 2.0 * np.pi * np.outer(np.arange(FOURIER_GROUP_DIM), np.arange(FOURIER_GROUP_DIM)) / FOURIER_GROUP_DIM
    cc = np.concatenate([np.cos(ac), np.sin(ac)], axis=0)
    return (jnp.asarray(cs1, dtype=BF16), jnp.asarray(twc, dtype=F32), jnp.asarray(tws, dtype=F32),
            jnp.asarray(f2, dtype=BF16), jnp.asarray(cc, dtype=BF16))


def _f1_body(u_ref, cs_ref, twc_ref, tws_ref, z_ref, *, n1, ncol):
    zc = jnp.dot(cs_ref[...], u_ref[...], preferred_element_type=F32)
    twc = twc_ref[0]
    tws = tws_ref[0]
    for j in range(F1_S2_TILE):
        zr = zc[0:n1, j * ncol:(j + 1) * ncol]
        zi = zc[n1:2 * n1, j * ncol:(j + 1) * ncol]
        c = twc[:, j:j + 1]
        s = tws[:, j:j + 1]
        z_ref[0, j] = (zr * c + zi * s).astype(BF16)
        z_ref[1, j] = (zi * c - zr * s).astype(BF16)


def _f2_body(z_ref, f_ref, cc_ref, y_ref, *, norm):
    n2 = DFT_S2
    gd = FOURIER_GROUP_DIM
    zz = z_ref[...].reshape(2 * n2, z_ref.shape[-1])
    xb = jnp.dot(f_ref[...], zz, preferred_element_type=F32).astype(BF16)
    for g in range(z_ref.shape[-1] // gd):
        lhs = jnp.concatenate([xb[0:n2, g * gd:(g + 1) * gd], xb[n2:2 * n2, g * gd:(g + 1) * gd]], axis=1)
        y = jnp.dot(lhs, cc_ref[...], preferred_element_type=F32)
        y_ref[:, g * gd:(g + 1) * gd] = (y * norm).astype(BF16)


def _fourier_mix(u, tables):
    s, ncol = u.shape
    n1, n2 = s // DFT_S2, DFT_S2
    cs1, twc, tws, f2, cc = tables
    ts2 = F1_S2_TILE
    z = pl.pallas_call(
        functools.partial(_f1_body, n1=n1, ncol=ncol),
        grid=(n2 // ts2,),
        in_specs=[
            pl.BlockSpec((n1, ts2 * ncol), lambda t: (0, t)),
            pl.BlockSpec((2 * n1, n1), lambda t: (0, 0)),
            pl.BlockSpec((1, n1, ts2), lambda t: (t, 0, 0)),
            pl.BlockSpec((1, n1, ts2), lambda t: (t, 0, 0)),
        ],
        out_specs=pl.BlockSpec((2, ts2, n1, ncol), lambda t: (0, t, 0, 0)),
        out_shape=jax.ShapeDtypeStruct((2, n2, n1, ncol), BF16),
        compiler_params=_params("parallel"),
        name="dft_stage1",
    )(u.reshape(n1, n2 * ncol), cs1, twc, tws)
    norm = 1.0 / math.sqrt(s * FOURIER_GROUP_DIM)
    kb = math.gcd(n1, F2_K1_PER_STEP)
    y = pl.pallas_call(
        functools.partial(_f2_body, norm=norm),
        grid=(n1 // kb,),
        in_specs=[
            pl.BlockSpec((2, n2, kb * ncol), lambda k: (0, 0, k)),
            pl.BlockSpec((2 * n2, 2 * n2), lambda k: (0, 0)),
            pl.BlockSpec((2 * FOURIER_GROUP_DIM, FOURIER_GROUP_DIM), lambda k: (0, 0)),
        ],
        out_specs=pl.BlockSpec((n2, kb * ncol), lambda k: (0, k)),
        out_shape=jax.ShapeDtypeStruct((n2, n1 * ncol), BF16),
        compiler_params=_params("parallel"),
        name="dft_stage2",
    )(z.reshape(2, n2, n1 * ncol), f2, cc)
    return y.reshape(s, ncol)


def _score_bound(gq, gk):
    bq = jnp.max(jnp.abs(gq), axis=-1, keepdims=True)
    bk = jnp.max(jnp.abs(gk), axis=-1, keepdims=True)
    return (QK_HEAD_DIM * (QK_HEAD_DIM ** -0.5) * LOG2_E) * bq * bk


def _for_each_q_tile(qt_ref, o_ref, tile_fn):
    tq = min(Q_TILE, qt_ref.shape[2])

    def sub(i, carry):
        off = pl.multiple_of(i * tq, tq)
        tile_fn(qt_ref[0, :, pl.ds(off, tq)], o_ref.at[pl.ds(off, tq), :])
        return carry

    lax.fori_loop(0, qt_ref.shape[2] // tq, sub, 0)


def _attn_prologue(qt):
    row = lax.broadcasted_iota(jnp.int32, qt.shape, 0)
    zero = jnp.zeros_like(qt)
    return jnp.where(row < QK_HEAD_DIM, qt, zero), jnp.where(row >= QK_HEAD_DIM, qt, zero)


def _attn_epilogue(o1, o2, lq1_ref, lk1_ref, lq2_ref, lk2_ref, gsub_ref, o_ref, lambda_init):
    lam = (jnp.exp(jnp.sum(lq1_ref[...] * lk1_ref[...], axis=-1, keepdims=True))
           - jnp.exp(jnp.sum(lq2_ref[...] * lk2_ref[...], axis=-1, keepdims=True)) + lambda_init)
    o = o1 - lam * o2
    ms = jnp.mean(o * o, axis=0, keepdims=True)
    y = o * lax.rsqrt(ms + SUBLN_EPS) * gsub_ref[...] * (1.0 - lambda_init)
    o_ref[...] = y.T.astype(BF16)


def _attn_online_body(qt_ref, k_ref, vt_ref, gq_ref, gk_ref, lq1_ref, lk1_ref, lq2_ref, lk2_ref, gsub_ref, o_ref,
                      m_ref, l_ref, acc_ref, *, lambda_init):
    del gq_ref, gk_ref
    nk, tk = vt_ref.shape[1], vt_ref.shape[3]

    def tile(qt, o_view):
        q_halves = _attn_prologue(qt)
        m_ref[...] = jnp.full(m_ref.shape, M_INIT, F32)
        l_ref[...] = jnp.zeros(l_ref.shape, F32)
        acc_ref[...] = jnp.zeros(acc_ref.shape, F32)

        def chunk(c, carry):
            off = pl.multiple_of(c * tk, tk)
            kc = k_ref[pl.ds(off, tk), :]
            vc = vt_ref[0, c]
            for hf in range(2):
                s = jnp.dot(kc, q_halves[hf], preferred_element_type=F32)
                m_old = m_ref[hf]
                m_new = jnp.maximum(m_old, jnp.max(s, axis=0, keepdims=True))
                alpha = jnp.exp2(m_old - m_new)
                p = jnp.exp2(s - m_new)
                l_ref[hf] = alpha * l_ref[hf] + jnp.sum(p, axis=0, keepdims=True)
                acc_ref[hf] = alpha * acc_ref[hf] + jnp.dot(vc, p.astype(BF16), preferred_element_type=F32)
                m_ref[hf] = m_new
            return carry

        lax.fori_loop(0, nk, chunk, 0)
        _attn_epilogue(acc_ref[0] / l_ref[0], acc_ref[1] / l_ref[1],
                       lq1_ref, lk1_ref, lq2_ref, lk2_ref, gsub_ref, o_view, lambda_init)

    _for_each_q_tile(qt_ref, o_ref, tile)


def _attn_bounded_body(qt_ref, k_ref, vt_ref, gq_ref, gk_ref, lq1_ref, lk1_ref, lq2_ref, lk2_ref, gsub_ref, o_ref,
                       l_ref, acc_ref, *, lambda_init):
    nk, tk = vt_ref.shape[1], vt_ref.shape[3]
    bound = _score_bound(gq_ref[...], gk_ref[...])

    per_chunk = tk // KV_UNIT
    unroll = math.gcd(nk, KV_UNROLL)
    units = [(cc, j, hf) for cc in range(unroll) for j in range(per_chunk) for hf in range(2)]

    def tile(qt, o_view):
        tq = qt.shape[1]
        q_halves = _attn_prologue(qt)
        l_ref[...] = jnp.zeros(l_ref.shape, F32)
        acc_ref[...] = jnp.zeros(acc_ref.shape, F32)

        def trip(t, carry):
            def scores(cc, j, hf):
                off = pl.multiple_of((t * unroll + cc) * tk + j * KV_UNIT, KV_UNIT)
                return jnp.dot(k_ref[pl.ds(off, KV_UNIT), :], q_halves[hf], preferred_element_type=F32)

            pending = [scores(*u) for u in units[:QK_LOOKAHEAD]]
            for n, (cc, j, hf) in enumerate(units):
                s = pending.pop(0)
                if n + QK_LOOKAHEAD < len(units):
                    pending.append(scores(*units[n + QK_LOOKAHEAD]))
                p = jnp.exp2(s - bound)
                l_ref[hf] += jnp.sum(p.reshape(KV_UNIT // 8, 8, tq), axis=0)
                vc = vt_ref[0, t * unroll + cc, :, j * KV_UNIT:(j + 1) * KV_UNIT]
                acc_ref[hf] += jnp.dot(vc, p.astype(BF16), preferred_element_type=F32)
            return carry

        lax.fori_loop(0, nk // unroll, trip, 0)
        l1 = jnp.sum(l_ref[0], axis=0, keepdims=True)
        l2 = jnp.sum(l_ref[1], axis=0, keepdims=True)
        _attn_epilogue(acc_ref[0] / l1, acc_ref[1] / l2,
                       lq1_ref, lk1_ref, lq2_ref, lk2_ref, gsub_ref, o_view, lambda_init)

    _for_each_q_tile(qt_ref, o_ref, tile)


def _diff_attention(qt, k, vt, gq, gk, lq1, lk1, lq2, lk2, gsub, lambda_init, online):
    h, _, s = qt.shape
    nk, tk = vt.shape[1], vt.shape[3]
    tq = min(Q_TILE, s)
    tqb = min(tq * Q_TILES_PER_STEP, s)
    vec = pl.BlockSpec((1, QK_HEAD_DIM), lambda hh, i: (0, 0))
    if online:
        body = _attn_online_body
        scratch = [pltpu.VMEM((2, 1, tq), F32), pltpu.VMEM((2, 1, tq), F32), pltpu.VMEM((2, V_HEAD_DIM, tq), F32)]
    else:
        body = _attn_bounded_body
        scratch = [pltpu.VMEM((2, 8, tq), F32), pltpu.VMEM((2, V_HEAD_DIM, tq), F32)]
    return pl.pallas_call(
        functools.partial(body, lambda_init=lambda_init),
        grid=(h, s // tqb),
        in_specs=[
            pl.BlockSpec((1, V_HEAD_DIM, tqb), lambda hh, i: (hh, 0, i)),
            pl.BlockSpec((s, V_HEAD_DIM), lambda hh, i: (0, hh)),
            pl.BlockSpec((1, nk, V_HEAD_DIM, tk), lambda hh, i: (hh, 0, 0, 0)),
            vec, vec, vec, vec, vec, vec,
            pl.BlockSpec((V_HEAD_DIM, 1), lambda hh, i: (0, 0)),
        ],
        out_specs=pl.BlockSpec((tqb, V_HEAD_DIM), lambda hh, i: (i, hh)),
        out_shape=jax.ShapeDtypeStruct((s, h * V_HEAD_DIM), BF16),
        scratch_shapes=scratch,
        compiler_params=_params("parallel", "parallel"),
        name="diff_attention_online" if online else "diff_attention_bounded",
    )(qt, k, vt, gq, gk, lq1, lk1, lq2, lk2, gsub)


def _outproj_body(x_ref, yf_ref, ya_ref, wf_ref, wa_ref, o_ref):
    o_ref[...] = (x_ref[...]
                  + jnp.dot(yf_ref[...], wf_ref[...], preferred_element_type=F32)
                  + jnp.dot(ya_ref[...], wa_ref[...], preferred_element_type=F32))


def _outproj(x, yf, ya, w_out, layer):
    s, d = x.shape
    nf, na = yf.shape[1], ya.shape[1]
    assert nf == na
    tm = min(ROW_TILE, s)
    return pl.pallas_call(
        _outproj_body,
        grid=(s // tm,),
        in_specs=[
            pl.BlockSpec((tm, d), lambda i: (i, 0)),
            pl.BlockSpec((tm, nf), lambda i: (i, 0)),
            pl.BlockSpec((tm, na), lambda i: (i, 0)),
            pl.BlockSpec((None, nf, d), lambda i: (layer, 0, 0)),
            pl.BlockSpec((None, na, d), lambda i: (layer, 1, 0)),
        ],
        out_specs=pl.BlockSpec((tm, d), lambda i: (i, 0)),
        out_shape=jax.ShapeDtypeStruct((s, d), F32),
        compiler_params=_params("parallel"),
        name="out_proj",
    )(x, yf, ya, w_out, w_out)


def kernel(x, positions, norm_ffn1, w1_gate, w1_up, w1_down, norm_mix, w_in, q_norm, k_norm,
           lambda_q1, lambda_k1, lambda_q2, lambda_k2, subln, w_out, norm_ffn2, w2_gate, w2_up, w2_down):
    b, s, d = x.shape
    depth = w_in.shape[0]
    assert b == 1 and s % DFT_S2 == 0
    xs = x.reshape(s, d)
    win, wout = w_in.astype(BF16), w_out.astype(BF16)

    cos, sina, sinb = _rope_tables(positions.reshape(s))
    gmat = _group_mean_matrix()
    tables = _dft_tables(s)
    q_scale = (QK_HEAD_DIM ** -0.5) * LOG2_E

    for l in range(depth):
        lambda_init = 0.8 - 0.6 * math.exp(-0.3 * l)
        xs = _ffn(xs, norm_ffn1[l].reshape(1, d), w1_gate, w1_up, w1_down, l)

        g_mix = norm_mix[l].reshape(1, d)
        gq = jnp.tile(q_norm[l], 2).reshape(1, V_HEAD_DIM)
        gk = jnp.tile(k_norm[l], 2).reshape(1, V_HEAD_DIM)
        u_f, qt, kk, vt = _in_proj(xs, g_mix, win, l, gq, gk, gmat, cos, sina, sinb, q_scale)
        y_f = _fourier_mix(u_f, tables)
        gq64, gk64 = q_norm[l].reshape(1, QK_HEAD_DIM), k_norm[l].reshape(1, QK_HEAD_DIM)
        attn_args = (qt, kk, vt, gq64, gk64,
                     lambda_q1[l].reshape(1, -1), lambda_k1[l].reshape(1, -1),
                     lambda_q2[l].reshape(1, -1), lambda_k2[l].reshape(1, -1),
                     subln[l].reshape(V_HEAD_DIM, 1))
        y_a = lax.cond(
            _score_bound(gq64, gk64)[0, 0] <= MAX_BOUNDED_SOFTMAX_LOG2,
            lambda *a: _diff_attention(*a, lambda_init, False),
            lambda *a: _diff_attention(*a, lambda_init, True),
            *attn_args)
        xs = _outproj(xs, y_f, y_a, wout, l)

        xs = _ffn(xs, norm_ffn2[l].reshape(1, d), w2_gate, w2_up, w2_down, l)
    return xs.reshape(b, s, d)
```

```python
import functools
import math

import numpy as np
import jax
import jax.numpy as jnp
from jax import lax
from jax.experimental import pallas as pl
from jax.experimental.pallas import tpu as pltpu

F32 = jnp.float32
BF16 = jnp.bfloat16

N_HEADS = 8
V_HEAD_DIM = 128
QK_HEAD_DIM = 64
N_FOURIER_GROUPS = 8
FOURIER_GROUP_DIM = 128
ROPE_THETA = 10000.0
NORM_EPS = 1e-6
SUBLN_EPS = 1e-5
LOG2_E = 1.4426950408889634

V7X_VMEM_BYTES = 64 * 1024 * 1024
VMEM_LIMIT_BYTES = V7X_VMEM_BYTES - 4 * 1024 * 1024
LANES = 128

ROW_TILE = 512
FFN_ROW_TILE = 1024
FF_TILE = 512
Q_TILE = 512
Q_TILES_PER_STEP = 8
KV_TILE = 512
DFT_S2 = 128
F1_S2_TILE = 8
F2_K1_PER_STEP = 4
KV_UNROLL = 16
KV_UNIT = 256
QK_LOOKAHEAD = 2
M_INIT = -1e30
MAX_BOUNDED_SOFTMAX_LOG2 = 60.0


def _params(*semantics):
    return pltpu.CompilerParams(dimension_semantics=semantics, vmem_limit_bytes=VMEM_LIMIT_BYTES)


def _rms_rows(x, g):
    ms = jnp.mean(x * x, axis=-1, keepdims=True)
    return x * lax.rsqrt(ms + NORM_EPS) * g


def _ffn_body(x_hbm, g_ref, wg_ref, wu_ref, wd_ref, o_ref, xrow_ref, xn_ref, x_sem):
    i, j = pl.program_id(0), pl.program_id(1)
    tm = o_ref.shape[0]

    def x_tile_copy(tile):
        return pltpu.make_async_copy(x_hbm.at[pl.ds(tile * tm, tm), :], xrow_ref, x_sem)

    @pl.when(jnp.logical_and(i == 0, j == 0))
    def _():
        x_tile_copy(0).start()

    @pl.when(j == 0)
    def _():
        x_tile_copy(i).wait()
        x = xrow_ref[...]
        xn_ref[...] = _rms_rows(x, g_ref[...]).astype(BF16)
        o_ref[...] = x

        @pl.when(i + 1 < pl.num_programs(0))
        def _():
            x_tile_copy(i + 1).start()

    xn = xn_ref[...]
    h = jnp.dot(xn, wg_ref[...].astype(BF16), preferred_element_type=F32)
    u = jnp.dot(xn, wu_ref[...].astype(BF16), preferred_element_type=F32)
    a = (h * jax.nn.sigmoid(h) * u).astype(BF16)
    o_ref[...] += 0.5 * jnp.dot(a, wd_ref[...].astype(BF16), preferred_element_type=F32)


def _ffn(x, g, wg, wu, wd, layer):
    s, d = x.shape
    f = wg.shape[-1]
    tm, tf = min(FFN_ROW_TILE, s), min(FF_TILE, f)
    return pl.pallas_call(
        _ffn_body,
        grid=(s // tm, f // tf),
        in_specs=[
            pl.BlockSpec(memory_space=pl.ANY),
            pl.BlockSpec((1, d), lambda i, j: (0, 0)),
            pl.BlockSpec((None, d, tf), lambda i, j: (layer, 0, j)),
            pl.BlockSpec((None, d, tf), lambda i, j: (layer, 0, j)),
            pl.BlockSpec((None, tf, d), lambda i, j: (layer, j, 0)),
        ],
        out_specs=pl.BlockSpec((tm, d), lambda i, j: (i, 0)),
        out_shape=jax.ShapeDtypeStruct((s, d), F32),
        scratch_shapes=[pltpu.VMEM((tm, d), F32), pltpu.VMEM((tm, d), BF16), pltpu.SemaphoreType.DMA(())],
        compiler_params=_params("arbitrary", "arbitrary"),
        name="ffn",
    )(x, g, wg, wu, wd)


def _rope_body(pos_ref, invf_ref, cos_ref, sina_ref, sinb_ref):
    ang = pos_ref[...] * invf_ref[...]
    c = jnp.cos(ang)
    s = jnp.sin(ang)
    lane = lax.broadcasted_iota(jnp.int32, ang.shape, 1)
    first = (lane & (QK_HEAD_DIM // 2)) == 0
    cos_ref[...] = c
    sina_ref[...] = jnp.where(first, -s, 0.0)
    sinb_ref[...] = jnp.where(first, 0.0, s)


def _rope_tables(positions):
    s = positions.shape[0]
    inv_freq = 1.0 / (ROPE_THETA ** (jnp.arange(0, QK_HEAD_DIM, 2, dtype=F32) / QK_HEAD_DIM))
    invf = jnp.tile(inv_freq, 2 * V_HEAD_DIM // QK_HEAD_DIM).reshape(1, V_HEAD_DIM)
    pos = positions.astype(F32).reshape(s, 1)
    ts = min(1024, s)
    tab = jax.ShapeDtypeStruct((s, V_HEAD_DIM), F32)
    return pl.pallas_call(
        _rope_body,
        grid=(s // ts,),
        in_specs=[pl.BlockSpec((ts, 1), lambda i: (i, 0)), pl.BlockSpec((1, V_HEAD_DIM), lambda i: (0, 0))],
        out_specs=[pl.BlockSpec((ts, V_HEAD_DIM), lambda i: (i, 0))] * 3,
        out_shape=[tab, tab, tab],
        compiler_params=_params("parallel"),
        name="rope_tables",
    )(pos, invf)


def _group_mean_matrix():
    idx = np.arange(V_HEAD_DIM) // QK_HEAD_DIM
    return jnp.asarray((idx[:, None] == idx[None, :]).astype(np.float32) / QK_HEAD_DIM, dtype=BF16)


def _qk_norm_rope(t, gain, gmat, cos, sina, sinb):
    sq = t * t
    hi = sq.astype(BF16)
    lo = (sq - hi.astype(F32)).astype(BF16)
    ms = jnp.dot(hi, gmat, preferred_element_type=F32) + jnp.dot(lo, gmat, preferred_element_type=F32)
    tn = t * lax.rsqrt(ms + NORM_EPS) * gain
    half = QK_HEAD_DIM // 2
    return tn * cos + pltpu.roll(tn, V_HEAD_DIM - half, 1) * sina + pltpu.roll(tn, half, 1) * sinb


def _in_proj_body(x_ref, g_ref, w_ref, gq_ref, gk_ref, gmat_ref, cos_ref, sina_ref, sinb_ref,
                  u_ref, qt_ref, k_ref, vt_ref, *, q_scale):
    n = N_HEADS * V_HEAD_DIM
    xn = _rms_rows(x_ref[...], g_ref[...]).astype(BF16)

    def group(c):
        return jnp.dot(xn, w_ref[:, c * n:(c + 1) * n], preferred_element_type=F32)

    u_ref[...] = group(0).astype(BF16)
    gmat, cos, sina, sinb = gmat_ref[...], cos_ref[...], sina_ref[...], sinb_ref[...]
    pq = group(1)
    for h in range(N_HEADS):
        t = _qk_norm_rope(pq[:, h * V_HEAD_DIM:(h + 1) * V_HEAD_DIM], gq_ref[...], gmat, cos, sina, sinb)
        qt_ref[h] = (t * q_scale).T.astype(BF16)
    pk = group(2)
    for h in range(N_HEADS):
        sl = slice(h * V_HEAD_DIM, (h + 1) * V_HEAD_DIM)
        k_ref[:, sl] = _qk_norm_rope(pk[:, sl], gk_ref[...], gmat, cos, sina, sinb).astype(BF16)
    pv = group(3)
    for h in range(N_HEADS):
        vt_ref[h, 0] = pv[:, h * V_HEAD_DIM:(h + 1) * V_HEAD_DIM].T.astype(BF16)


def _in_proj(x, g, w_in, layer, gq, gk, gmat, cos, sina, sinb, q_scale):
    s, d = x.shape
    n = N_HEADS * V_HEAD_DIM
    tm = min(ROW_TILE, s)
    assert tm == min(KV_TILE, s) and w_in.shape[-1] == 4 * n
    row_tab = pl.BlockSpec((tm, V_HEAD_DIM), lambda i: (i, 0))
    head_vec = pl.BlockSpec((1, V_HEAD_DIM), lambda i: (0, 0))
    return pl.pallas_call(
        functools.partial(_in_proj_body, q_scale=q_scale),
        grid=(s // tm,),
        in_specs=[
            pl.BlockSpec((tm, d), lambda i: (i, 0)),
            pl.BlockSpec((1, d), lambda i: (0, 0)),
            pl.BlockSpec((None, d, 4 * n), lambda i: (layer, 0, 0)),
            head_vec, head_vec,
            pl.BlockSpec((V_HEAD_DIM, V_HEAD_DIM), lambda i: (0, 0)),
            row_tab, row_tab, row_tab,
        ],
        out_specs=[
            pl.BlockSpec((tm, n), lambda i: (i, 0)),
            pl.BlockSpec((N_HEADS, V_HEAD_DIM, tm), lambda i: (0, 0, i)),
            pl.BlockSpec((tm, n), lambda i: (i, 0)),
            pl.BlockSpec((N_HEADS, 1, V_HEAD_DIM, tm), lambda i: (0, i, 0, 0)),
        ],
        out_shape=[
            jax.ShapeDtypeStruct((s, n), BF16),
            jax.ShapeDtypeStruct((N_HEADS, V_HEAD_DIM, s), BF16),
            jax.ShapeDtypeStruct((s, n), BF16),
            jax.ShapeDtypeStruct((N_HEADS, s // tm, V_HEAD_DIM, tm), BF16),
        ],
        compiler_params=_params("parallel"),
        name="in_proj",
    )(x, g, w_in, gq, gk, gmat, cos, sina, sinb)


def _dft_tables(s):
    n1, n2 = s // DFT_S2, DFT_S2
    a1 = 2.0 * np.pi * np.outer(np.arange(n1), np.arange(n1)) / n1
    cs1 = np.concatenate([np.cos(a1), -np.sin(a1)], axis=0)
    at = 2.0 * np.pi * np.outer(np.arange(n1), np.arange(n2)) / s
    nt = n2 // F1_S2_TILE
    twc = np.cos(at).reshape(n1, nt, F1_S2_TILE).transpose(1, 0, 2)
    tws = np.sin(at).reshape(n1, nt, F1_S2_TILE).transpose(1, 0, 2)
    a2 = 2.0 * np.pi * np.outer(np.arange(n2), np.arange(n2)) / n2
    c2, s2 = np.cos(a2), np.sin(a2)
    f2 = np.block([[c2, s2], [-s2, c2]])
    ac = 2.0 * np.pi * np.outer(np.arange(FOURIER_GROUP_DIM), np.arange(FOURIER_GROUP_DIM)) / FOURIER_GROUP_DIM
    cc = np.concatenate([np.cos(ac), np.sin(ac)], axis=0)
    return (jnp.asarray(cs1, dtype=BF16), jnp.asarray(twc, dtype=F32), jnp.asarray(tws, dtype=F32),
            jnp.asarray(f2, dtype=BF16), jnp.asarray(cc, dtype=BF16))


def _f1_body(u_ref, cs_ref, twc_ref, tws_ref, z_ref, *, n1, ncol):
    zc = jnp.dot(cs_ref[...], u_ref[...], preferred_element_type=F32)
    twc = twc_ref[0]
    tws = tws_ref[0]
    for j in range(F1_S2_TILE):
        zr = zc[0:n1, j * ncol:(j + 1) * ncol]
        zi = zc[n1:2 * n1, j * ncol:(j + 1) * ncol]
        c = twc[:, j:j + 1]
        s = tws[:, j:j + 1]
        z_ref[0, j] = (zr * c + zi * s).astype(BF16)
        z_ref[1, j] = (zi * c - zr * s).astype(BF16)


def _f2_body(z_ref, f_ref, cc_ref, y_ref, *, norm):
    n2 = DFT_S2
    gd = FOURIER_GROUP_DIM
    zz = z_ref[...].reshape(2 * n2, z_ref.shape[-1])
    xb = jnp.dot(f_ref[...], zz, preferred_element_type=F32).astype(BF16)
    for g in range(z_ref.shape[-1] // gd):
        lhs = jnp.concatenate([xb[0:n2, g * gd:(g + 1) * gd], xb[n2:2 * n2, g * gd:(g + 1) * gd]], axis=1)
        y = jnp.dot(lhs, cc_ref[...], preferred_element_type=F32)
        y_ref[:, g * gd:(g + 1) * gd] = (y * norm).astype(BF16)


def _fourier_mix(u, tables):
    s, ncol = u.shape
    n1, n2 = s // DFT_S2, DFT_S2
    cs1, twc, tws, f2, cc = tables
    ts2 = F1_S2_TILE
    z = pl.pallas_call(
        functools.partial(_f1_body, n1=n1, ncol=ncol),
        grid=(n2 // ts2,),
        in_specs=[
            pl.BlockSpec((n1, ts2 * ncol), lambda t: (0, t)),
            pl.BlockSpec((2 * n1, n1), lambda t: (0, 0)),
            pl.BlockSpec((1, n1, ts2), lambda t: (t, 0, 0)),
            pl.BlockSpec((1, n1, ts2), lambda t: (t, 0, 0)),
        ],
        out_specs=pl.BlockSpec((2, ts2, n1, ncol), lambda t: (0, t, 0, 0)),
        out_shape=jax.ShapeDtypeStruct((2, n2, n1, ncol), BF16),
        compiler_params=_params("parallel"),
        name="dft_stage1",
    )(u.reshape(n1, n2 * ncol), cs1, twc, tws)
    norm = 1.0 / math.sqrt(s * FOURIER_GROUP_DIM)
    kb = math.gcd(n1, F2_K1_PER_STEP)
    y = pl.pallas_call(
        functools.partial(_f2_body, norm=norm),
        grid=(n1 // kb,),
        in_specs=[
            pl.BlockSpec((2, n2, kb * ncol), lambda k: (0, 0, k)),
            pl.BlockSpec((2 * n2, 2 * n2), lambda k: (0, 0)),
            pl.BlockSpec((2 * FOURIER_GROUP_DIM, FOURIER_GROUP_DIM), lambda k: (0, 0)),
        ],
        out_specs=pl.BlockSpec((n2, kb * ncol), lambda k: (0, k)),
        out_shape=jax.ShapeDtypeStruct((n2, n1 * ncol), BF16),
        compiler_params=_params("parallel"),
        name="dft_stage2",
    )(z.reshape(2, n2, n1 * ncol), f2, cc)
    return y.reshape(s, ncol)


def _score_bound(gq, gk):
    bq = jnp.max(jnp.abs(gq), axis=-1, keepdims=True)
    bk = jnp.max(jnp.abs(gk), axis=-1, keepdims=True)
    return (QK_HEAD_DIM * (QK_HEAD_DIM ** -0.5) * LOG2_E) * bq * bk


def _for_each_q_tile(qt_ref, o_ref, tile_fn):
    tq = min(Q_TILE, qt_ref.shape[2])

    def sub(i, carry):
        off = pl.multiple_of(i * tq, tq)
        tile_fn(qt_ref[0, :, pl.ds(off, tq)], o_ref.at[pl.ds(off, tq), :])
        return carry

    lax.fori_loop(0, qt_ref.shape[2] // tq, sub, 0)


def _attn_prologue(qt):
    row = lax.broadcasted_iota(jnp.int32, qt.shape, 0)
    zero = jnp.zeros_like(qt)
    return jnp.where(row < QK_HEAD_DIM, qt, zero), jnp.where(row >= QK_HEAD_DIM, qt, zero)


def _attn_epilogue(o1, o2, lq1_ref, lk1_ref, lq2_ref, lk2_ref, gsub_ref, o_ref, lambda_init):
    lam = (jnp.exp(jnp.sum(lq1_ref[...] * lk1_ref[...], axis=-1, keepdims=True))
           - jnp.exp(jnp.sum(lq2_ref[...] * lk2_ref[...], axis=-1, keepdims=True)) + lambda_init)
    o = o1 - lam * o2
    ms = jnp.mean(o * o, axis=0, keepdims=True)
    y = o * lax.rsqrt(ms + SUBLN_EPS) * gsub_ref[...] * (1.0 - lambda_init)
    o_ref[...] = y.T.astype(BF16)


def _attn_online_body(qt_ref, k_ref, vt_ref, gq_ref, gk_ref, lq1_ref, lk1_ref, lq2_ref, lk2_ref, gsub_ref, o_ref,
                      m_ref, l_ref, acc_ref, *, lambda_init):
    del gq_ref, gk_ref
    nk, tk = vt_ref.shape[1], vt_ref.shape[3]

    def tile(qt, o_view):
        q_halves = _attn_prologue(qt)
        m_ref[...] = jnp.full(m_ref.shape, M_INIT, F32)
        l_ref[...] = jnp.zeros(l_ref.shape, F32)
        acc_ref[...] = jnp.zeros(acc_ref.shape, F32)

        def chunk(c, carry):
            off = pl.multiple_of(c * tk, tk)
            kc = k_ref[pl.ds(off, tk), :]
            vc = vt_ref[0, c]
            for hf in range(2):
                s = jnp.dot(kc, q_halves[hf], preferred_element_type=F32)
                m_old = m_ref[hf]
                m_new = jnp.maximum(m_old, jnp.max(s, axis=0, keepdims=True))
                alpha = jnp.exp2(m_old - m_new)
                p = jnp.exp2(s - m_new)
                l_ref[hf] = alpha * l_ref[hf] + jnp.sum(p, axis=0, keepdims=True)
                acc_ref[hf] = alpha * acc_ref[hf] + jnp.dot(vc, p.astype(BF16), preferred_element_type=F32)
                m_ref[hf] = m_new
            return carry

        lax.fori_loop(0, nk, chunk, 0)
        _attn_epilogue(acc_ref[0] / l_ref[0], acc_ref[1] / l_ref[1],
                       lq1_ref, lk1_ref, lq2_ref, lk2_ref, gsub_ref, o_view, lambda_init)

    _for_each_q_tile(qt_ref, o_ref, tile)


def _attn_bounded_body(qt_ref, k_ref, vt_ref, gq_ref, gk_ref, lq1_ref, lk1_ref, lq2_ref, lk2_ref, gsub_ref, o_ref,
                       l_ref, acc_ref, *, lambda_init):
    nk, tk = vt_ref.shape[1], vt_ref.shape[3]
    bound = _score_bound(gq_ref[...], gk_ref[...])

    per_chunk = tk // KV_UNIT
    unroll = math.gcd(nk, KV_UNROLL)
    units = [(cc, j, hf) for cc in range(unroll) for j in range(per_chunk) for hf in range(2)]

    tq = min(Q_TILE, qt_ref.shape[2])
    n_tiles = qt_ref.shape[2] // tq

    def q_rows(i):
        return pl.ds(i * tq if isinstance(i, int) else pl.multiple_of(i * tq, tq), tq)

    def accumulate(i):
        q_halves = _attn_prologue(qt_ref[0, :, q_rows(i)])

        def trip(t, first_trip):
            def scores(cc, j, hf):
                off = pl.multiple_of((t * unroll + cc) * tk + j * KV_UNIT, KV_UNIT)
                return jnp.dot(k_ref[pl.ds(off, KV_UNIT), :], q_halves[hf], preferred_element_type=F32)

            pending = [scores(*u) for u in units[:QK_LOOKAHEAD]]
            for n, (cc, j, hf) in enumerate(units):
                s = pending.pop(0)
                if n + QK_LOOKAHEAD < len(units):
                    pending.append(scores(*units[n + QK_LOOKAHEAD]))
                p = jnp.exp2(s - bound)
                psum = jnp.sum(p.reshape(KV_UNIT // 8, 8, tq), axis=0)
                vc = vt_ref[0, t * unroll + cc, :, j * KV_UNIT:(j + 1) * KV_UNIT]
                pv = jnp.dot(vc, p.astype(BF16), preferred_element_type=F32)
                if first_trip and n < 2:
                    l_ref[hf] = psum
                    acc_ref[hf] = pv
                else:
                    l_ref[hf] += psum
                    acc_ref[hf] += pv

        trip(0, True)
        if nk // unroll > 1:
            lax.fori_loop(1, nk // unroll, lambda t, c: (trip(t, False), c)[1], 0)

    def finish(i):
        l1 = jnp.sum(l_ref[0], axis=0, keepdims=True)
        l2 = jnp.sum(l_ref[1], axis=0, keepdims=True)
        _attn_epilogue(acc_ref[0] / l1, acc_ref[1] / l2, lq1_ref, lk1_ref, lq2_ref, lk2_ref, gsub_ref,
                       o_ref.at[q_rows(i), :], lambda_init)

    accumulate(0)

    def step(i, carry):
        finish(i - 1)
        accumulate(i)
        return carry

    lax.fori_loop(1, n_tiles, step, 0)
    finish(n_tiles - 1)


def _diff_attention(qt, k, vt, gq, gk, lq1, lk1, lq2, lk2, gsub, lambda_init, online):
    h, _, s = qt.shape
    nk, tk = vt.shape[1], vt.shape[3]
    tq = min(Q_TILE, s)
    tqb = min(tq * Q_TILES_PER_STEP, s)
    vec = pl.BlockSpec((1, QK_HEAD_DIM), lambda hh, i: (0, 0))
    if online:
        body = _attn_online_body
        scratch = [pltpu.VMEM((2, 1, tq), F32), pltpu.VMEM((2, 1, tq), F32), pltpu.VMEM((2, V_HEAD_DIM, tq), F32)]
    else:
        body = _attn_bounded_body
        scratch = [pltpu.VMEM((2, 8, tq), F32), pltpu.VMEM((2, V_HEAD_DIM, tq), F32)]
    return pl.pallas_call(
        functools.partial(body, lambda_init=lambda_init),
        grid=(h, s // tqb),
        in_specs=[
            pl.BlockSpec((1, V_HEAD_DIM, tqb), lambda hh, i: (hh, 0, i)),
            pl.BlockSpec((s, V_HEAD_DIM), lambda hh, i: (0, hh)),
            pl.BlockSpec((1, nk, V_HEAD_DIM, tk), lambda hh, i: (hh, 0, 0, 0)),
            vec, vec, vec, vec, vec, vec,
            pl.BlockSpec((V_HEAD_DIM, 1), lambda hh, i: (0, 0)),
        ],
        out_specs=pl.BlockSpec((tqb, V_HEAD_DIM), lambda hh, i: (i, hh)),
        out_shape=jax.ShapeDtypeStruct((s, h * V_HEAD_DIM), BF16),
        scratch_shapes=scratch,
        compiler_params=_params("parallel", "parallel"),
        name="diff_attention_online" if online else "diff_attention_bounded",
    )(qt, k, vt, gq, gk, lq1, lk1, lq2, lk2, gsub)


def _outproj_body(x_ref, yf_ref, ya_ref, wf_ref, wa_ref, o_ref):
    o_ref[...] = (x_ref[...]
                  + jnp.dot(yf_ref[...], wf_ref[...], preferred_element_type=F32)
                  + jnp.dot(ya_ref[...], wa_ref[...], preferred_element_type=F32))


def _outproj(x, yf, ya, w_out, layer):
    s, d = x.shape
    nf, na = yf.shape[1], ya.shape[1]
    assert nf == na
    tm = min(ROW_TILE, s)
    return pl.pallas_call(
        _outproj_body,
        grid=(s // tm,),
        in_specs=[
            pl.BlockSpec((tm, d), lambda i: (i, 0)),
            pl.BlockSpec((tm, nf), lambda i: (i, 0)),
            pl.BlockSpec((tm, na), lambda i: (i, 0)),
            pl.BlockSpec((None, nf, d), lambda i: (layer, 0, 0)),
            pl.BlockSpec((None, na, d), lambda i: (layer, 1, 0)),
        ],
        out_specs=pl.BlockSpec((tm, d), lambda i: (i, 0)),
        out_shape=jax.ShapeDtypeStruct((s, d), F32),
        compiler_params=_params("parallel"),
        name="out_proj",
    )(x, yf, ya, w_out, w_out)


def kernel(x, positions, norm_ffn1, w1_gate, w1_up, w1_down, norm_mix, w_in, q_norm, k_norm,
           lambda_q1, lambda_k1, lambda_q2, lambda_k2, subln, w_out, norm_ffn2, w2_gate, w2_up, w2_down):
    b, s, d = x.shape
    depth = w_in.shape[0]
    assert b == 1 and s % DFT_S2 == 0
    xs = x.reshape(s, d)
    win, wout = w_in.astype(BF16), w_out.astype(BF16)

    cos, sina, sinb = _rope_tables(positions.reshape(s))
    gmat = _group_mean_matrix()
    tables = _dft_tables(s)
    q_scale = (QK_HEAD_DIM ** -0.5) * LOG2_E

    for l in range(depth):
        lambda_init = 0.8 - 0.6 * math.exp(-0.3 * l)
        xs = _ffn(xs, norm_ffn1[l].reshape(1, d), w1_gate, w1_up, w1_down, l)

        g_mix = norm_mix[l].reshape(1, d)
        gq = jnp.tile(q_norm[l], 2).reshape(1, V_HEAD_DIM)
        gk = jnp.tile(k_norm[l], 2).reshape(1, V_HEAD_DIM)
        u_f, qt, kk, vt = _in_proj(xs, g_mix, win, l, gq, gk, gmat, cos, sina, sinb, q_scale)
        y_f = _fourier_mix(u_f, tables)
        gq64, gk64 = q_norm[l].reshape(1, QK_HEAD_DIM), k_norm[l].reshape(1, QK_HEAD_DIM)
        attn_args = (qt, kk, vt, gq64, gk64,
                     lambda_q1[l].reshape(1, -1), lambda_k1[l].reshape(1, -1),
                     lambda_q2[l].reshape(1, -1), lambda_k2[l].reshape(1, -1),
                     subln[l].reshape(V_HEAD_DIM, 1))
        y_a = lax.cond(
            _score_bound(gq64, gk64)[0, 0] <= MAX_BOUNDED_SOFTMAX_LOG2,
            lambda *a: _diff_attention(*a, lambda_init, False),
            lambda *a: _diff_attention(*a, lambda_init, True),
            *attn_args)
        xs = _outproj(xs, y_f, y_a, wout, l)

        xs = _ffn(xs, norm_ffn2[l].reshape(1, d), w2_gate, w2_up, w2_down, l)
    return xs.reshape(b, s, d)
```

```python
import functools
import math

import numpy as np
import jax
import jax.numpy as jnp
from jax import lax
from jax.experimental import pallas as pl
from jax.experimental.pallas import tpu as pltpu

F32 = jnp.float32
BF16 = jnp.bfloat16

N_HEADS = 8
V_HEAD_DIM = 128
QK_HEAD_DIM = 64
N_FOURIER_GROUPS = 8
FOURIER_GROUP_DIM = 128
ROPE_THETA = 10000.0
NORM_EPS = 1e-6
SUBLN_EPS = 1e-5
LOG2_E = 1.4426950408889634

V7X_VMEM_BYTES = 64 * 1024 * 1024
VMEM_LIMIT_BYTES = V7X_VMEM_BYTES - 4 * 1024 * 1024
LANES = 128

ROW_TILE = 512
FFN_ROW_TILE = 1024
FF_TILE = 512
Q_TILE = 512
Q_TILES_PER_STEP = 8
KV_TILE = 512
DFT_S2 = 128
F1_S2_TILE = 8
F2_K1_PER_STEP = 4
KV_UNROLL = 16
KV_UNIT = 256
QK_LOOKAHEAD = 2
M_INIT = -1e30
MAX_BOUNDED_SOFTMAX_LOG2 = 60.0


def _params(*semantics):
    return pltpu.CompilerParams(dimension_semantics=semantics, vmem_limit_bytes=VMEM_LIMIT_BYTES)


def _rms_rows(x, g):
    ms = jnp.mean(x * x, axis=-1, keepdims=True)
    return x * lax.rsqrt(ms + NORM_EPS) * g


def _ffn_body(x_hbm, g_ref, wg_hbm, wu_hbm, wd_hbm, o_ref, xrow_ref, xn_ref, wg_buf, wu_buf, wd_buf, x_sem, w_sem,
              *, layer, tf):
    i, n_i = pl.program_id(0), pl.num_programs(0)
    tm = o_ref.shape[0]
    nf = wg_hbm.shape[2] // tf

    def x_tile_copy(tile):
        return pltpu.make_async_copy(x_hbm.at[pl.ds(tile * tm, tm), :], xrow_ref, x_sem)

    def weight_copies(chunk, slot):
        cols = pl.ds(pl.multiple_of(chunk * tf, tf), tf)
        return (pltpu.make_async_copy(wg_hbm.at[layer, :, cols], wg_buf.at[slot], w_sem.at[0, slot]),
                pltpu.make_async_copy(wu_hbm.at[layer, :, cols], wu_buf.at[slot], w_sem.at[1, slot]),
                pltpu.make_async_copy(wd_hbm.at[layer, cols, :], wd_buf.at[slot], w_sem.at[2, slot]))

    @pl.when(i == 0)
    def _():
        x_tile_copy(0).start()
        for cp in weight_copies(0, 0):
            cp.start()

    x_tile_copy(i).wait()
    x = xrow_ref[...]
    xn_ref[...] = _rms_rows(x, g_ref[...]).astype(BF16)
    o_ref[...] = x

    @pl.when(i + 1 < n_i)
    def _():
        x_tile_copy(i + 1).start()

    def chunk(c, carry):
        step = i * nf + c
        slot = lax.rem(step, 2)
        for cp in weight_copies(c, slot):
            cp.wait()

        @pl.when(step + 1 < n_i * nf)
        def _():
            for cp in weight_copies(jnp.where(c + 1 == nf, 0, c + 1), 1 - slot):
                cp.start()

        xn = xn_ref[...]
        h = jnp.dot(xn, wg_buf[slot].astype(BF16), preferred_element_type=F32)
        u = jnp.dot(xn, wu_buf[slot].astype(BF16), preferred_element_type=F32)
        a = (h * jax.nn.sigmoid(h) * u).astype(BF16)
        o_ref[...] += 0.5 * jnp.dot(a, wd_buf[slot].astype(BF16), preferred_element_type=F32)
        return carry

    lax.fori_loop(0, nf, chunk, 0)


def _ffn(x, g, wg, wu, wd, layer):
    s, d = x.shape
    f = wg.shape[-1]
    tm, tf = min(FFN_ROW_TILE, s), min(FF_TILE, f)
    assert s % tm == 0 and f % tf == 0
    hbm = pl.BlockSpec(memory_space=pl.ANY)
    return pl.pallas_call(
        functools.partial(_ffn_body, layer=layer, tf=tf),
        grid=(s // tm,),
        in_specs=[hbm, pl.BlockSpec((1, d), lambda i: (0, 0)), hbm, hbm, hbm],
        out_specs=pl.BlockSpec((tm, d), lambda i: (i, 0)),
        out_shape=jax.ShapeDtypeStruct((s, d), F32),
        scratch_shapes=[
            pltpu.VMEM((tm, d), F32),
            pltpu.VMEM((tm, d), BF16),
            pltpu.VMEM((2, d, tf), F32),
            pltpu.VMEM((2, d, tf), F32),
            pltpu.VMEM((2, tf, d), F32),
            pltpu.SemaphoreType.DMA(()),
            pltpu.SemaphoreType.DMA((3, 2)),
        ],
        compiler_params=_params("arbitrary"),
        name="ffn",
    )(x, g, wg, wu, wd)


def _rope_body(pos_ref, invf_ref, cos_ref, sina_ref, sinb_ref):
    ang = pos_ref[...] * invf_ref[...]
    c = jnp.cos(ang)
    s = jnp.sin(ang)
    lane = lax.broadcasted_iota(jnp.int32, ang.shape, 1)
    first = (lane & (QK_HEAD_DIM // 2)) == 0
    cos_ref[...] = c
    sina_ref[...] = jnp.where(first, -s, 0.0)
    sinb_ref[...] = jnp.where(first, 0.0, s)


def _rope_tables(positions):
    s = positions.shape[0]
    inv_freq = 1.0 / (ROPE_THETA ** (jnp.arange(0, QK_HEAD_DIM, 2, dtype=F32) / QK_HEAD_DIM))
    invf = jnp.tile(inv_freq, 2 * V_HEAD_DIM // QK_HEAD_DIM).reshape(1, V_HEAD_DIM)
    pos = positions.astype(F32).reshape(s, 1)
    ts = min(1024, s)
    tab = jax.ShapeDtypeStruct((s, V_HEAD_DIM), F32)
    return pl.pallas_call(
        _rope_body,
        grid=(s // ts,),
        in_specs=[pl.BlockSpec((ts, 1), lambda i: (i, 0)), pl.BlockSpec((1, V_HEAD_DIM), lambda i: (0, 0))],
        out_specs=[pl.BlockSpec((ts, V_HEAD_DIM), lambda i: (i, 0))] * 3,
        out_shape=[tab, tab, tab],
        compiler_params=_params("parallel"),
        name="rope_tables",
    )(pos, invf)


def _group_mean_matrix():
    idx = np.arange(V_HEAD_DIM) // QK_HEAD_DIM
    return jnp.asarray((idx[:, None] == idx[None, :]).astype(np.float32) / QK_HEAD_DIM, dtype=BF16)


def _qk_norm_rope(t, gain, gmat, cos, sina, sinb):
    sq = t * t
    hi = sq.astype(BF16)
    lo = (sq - hi.astype(F32)).astype(BF16)
    ms = jnp.dot(hi, gmat, preferred_element_type=F32) + jnp.dot(lo, gmat, preferred_element_type=F32)
    tn = t * lax.rsqrt(ms + NORM_EPS) * gain
    half = QK_HEAD_DIM // 2
    return tn * cos + pltpu.roll(tn, V_HEAD_DIM - half, 1) * sina + pltpu.roll(tn, half, 1) * sinb


def _in_proj_body(x_ref, g_ref, w_ref, gq_ref, gk_ref, gmat_ref, cos_ref, sina_ref, sinb_ref,
                  u_ref, qt_ref, k_ref, vt_ref, *, q_scale):
    n = N_HEADS * V_HEAD_DIM
    xn = _rms_rows(x_ref[...], g_ref[...]).astype(BF16)

    def group(c):
        return jnp.dot(xn, w_ref[:, c * n:(c + 1) * n], preferred_element_type=F32)

    u_ref[...] = group(0).astype(BF16)
    gmat, cos, sina, sinb = gmat_ref[...], cos_ref[...], sina_ref[...], sinb_ref[...]
    pq = group(1)
    for h in range(N_HEADS):
        t = _qk_norm_rope(pq[:, h * V_HEAD_DIM:(h + 1) * V_HEAD_DIM], gq_ref[...], gmat, cos, sina, sinb)
        qt_ref[h] = (t * q_scale).T.astype(BF16)
    pk = group(2)
    for h in range(N_HEADS):
        sl = slice(h * V_HEAD_DIM, (h + 1) * V_HEAD_DIM)
        k_ref[:, sl] = _qk_norm_rope(pk[:, sl], gk_ref[...], gmat, cos, sina, sinb).astype(BF16)
    pv = group(3)
    for h in range(N_HEADS):
        vt_ref[h, 0] = pv[:, h * V_HEAD_DIM:(h + 1) * V_HEAD_DIM].T.astype(BF16)


def _in_proj(x, g, w_in, layer, gq, gk, gmat, cos, sina, sinb, q_scale):
    s, d = x.shape
    n = N_HEADS * V_HEAD_DIM
    tm = min(ROW_TILE, s)
    assert tm == min(KV_TILE, s) and w_in.shape[-1] == 4 * n
    row_tab = pl.BlockSpec((tm, V_HEAD_DIM), lambda i: (i, 0))
    head_vec = pl.BlockSpec((1, V_HEAD_DIM), lambda i: (0, 0))
    return pl.pallas_call(
        functools.partial(_in_proj_body, q_scale=q_scale),
        grid=(s // tm,),
        in_specs=[
            pl.BlockSpec((tm, d), lambda i: (i, 0)),
            pl.BlockSpec((1, d), lambda i: (0, 0)),
            pl.BlockSpec((None, d, 4 * n), lambda i: (layer, 0, 0)),
            head_vec, head_vec,
            pl.BlockSpec((V_HEAD_DIM, V_HEAD_DIM), lambda i: (0, 0)),
            row_tab, row_tab, row_tab,
        ],
        out_specs=[
            pl.BlockSpec((tm, n), lambda i: (i, 0)),
            pl.BlockSpec((N_HEADS, V_HEAD_DIM, tm), lambda i: (0, 0, i)),
            pl.BlockSpec((tm, n), lambda i: (i, 0)),
            pl.BlockSpec((N_HEADS, 1, V_HEAD_DIM, tm), lambda i: (0, i, 0, 0)),
        ],
        out_shape=[
            jax.ShapeDtypeStruct((s, n), BF16),
            jax.ShapeDtypeStruct((N_HEADS, V_HEAD_DIM, s), BF16),
            jax.ShapeDtypeStruct((s, n), BF16),
            jax.ShapeDtypeStruct((N_HEADS, s // tm, V_HEAD_DIM, tm), BF16),
        ],
        compiler_params=_params("parallel"),
        name="in_proj",
    )(x, g, w_in, gq, gk, gmat, cos, sina, sinb)


def _dft_tables(s):
    n1, n2 = s // DFT_S2, DFT_S2
    a1 = 2.0 * np.pi * np.outer(np.arange(n1), np.arange(n1)) / n1
    cs1 = np.concatenate([np.cos(a1), -np.sin(a1)], axis=0)
    at = 2.0 * np.pi * np.outer(np.arange(n1), np.arange(n2)) / s
    nt = n2 // F1_S2_TILE
    twc = np.cos(at).reshape(n1, nt, F1_S2_TILE).transpose(1, 0, 2)
    tws = np.sin(at).reshape(n1, nt, F1_S2_TILE).transpose(1, 0, 2)
    a2 = 2.0 * np.pi * np.outer(np.arange(n2), np.arange(n2)) / n2
    c2, s2 = np.cos(a2), np.sin(a2)
    f2 = np.block([[c2, s2], [-s2, c2]])
    ac = 2.0 * np.pi * np.outer(np.arange(FOURIER_GROUP_DIM), np.arange(FOURIER_GROUP_DIM)) / FOURIER_GROUP_DIM
    cc = np.concatenate([np.cos(ac), np.sin(ac)], axis=0)
    return (jnp.asarray(cs1, dtype=BF16), jnp.asarray(twc, dtype=F32), jnp.asarray(tws, dtype=F32),
            jnp.asarray(f2, dtype=BF16), jnp.asarray(cc, dtype=BF16))


def _f1_body(u_ref, cs_ref, twc_ref, tws_ref, z_ref, *, n1, ncol):
    zc = jnp.dot(cs_ref[...], u_ref[...], preferred_element_type=F32)
    twc = twc_ref[0]
    tws = tws_ref[0]
    for j in range(F1_S2_TILE):
        zr = zc[0:n1, j * ncol:(j + 1) * ncol]
        zi = zc[n1:2 * n1, j * ncol:(j + 1) * ncol]
        c = twc[:, j:j + 1]
        s = tws[:, j:j + 1]
        z_ref[0, j] = (zr * c + zi * s).astype(BF16)
        z_ref[1, j] = (zi * c - zr * s).astype(BF16)


def _f2_body(z_ref, f_ref, cc_ref, y_ref, *, norm):
    n2 = DFT_S2
    gd = FOURIER_GROUP_DIM
    zz = z_ref[...].reshape(2 * n2, z_ref.shape[-1])
    xb = jnp.dot(f_ref[...], zz, preferred_element_type=F32).astype(BF16)
    for g in range(z_ref.shape[-1] // gd):
        lhs = jnp.concatenate([xb[0:n2, g * gd:(g + 1) * gd], xb[n2:2 * n2, g * gd:(g + 1) * gd]], axis=1)
        y = jnp.dot(lhs, cc_ref[...], preferred_element_type=F32)
        y_ref[:, g * gd:(g + 1) * gd] = (y * norm).astype(BF16)


def _fourier_mix(u, tables):
    s, ncol = u.shape
    n1, n2 = s // DFT_S2, DFT_S2
    cs1, twc, tws, f2, cc = tables
    ts2 = F1_S2_TILE
    z = pl.pallas_call(
        functools.partial(_f1_body, n1=n1, ncol=ncol),
        grid=(n2 // ts2,),
        in_specs=[
            pl.BlockSpec((n1, ts2 * ncol), lambda t: (0, t)),
            pl.BlockSpec((2 * n1, n1), lambda t: (0, 0)),
            pl.BlockSpec((1, n1, ts2), lambda t: (t, 0, 0)),
            pl.BlockSpec((1, n1, ts2), lambda t: (t, 0, 0)),
        ],
        out_specs=pl.BlockSpec((2, ts2, n1, ncol), lambda t: (0, t, 0, 0)),
        out_shape=jax.ShapeDtypeStruct((2, n2, n1, ncol), BF16),
        compiler_params=_params("parallel"),
        name="dft_stage1",
    )(u.reshape(n1, n2 * ncol), cs1, twc, tws)
    norm = 1.0 / math.sqrt(s * FOURIER_GROUP_DIM)
    kb = math.gcd(n1, F2_K1_PER_STEP)
    y = pl.pallas_call(
        functools.partial(_f2_body, norm=norm),
        grid=(n1 // kb,),
        in_specs=[
            pl.BlockSpec((2, n2, kb * ncol), lambda k: (0, 0, k)),
            pl.BlockSpec((2 * n2, 2 * n2), lambda k: (0, 0)),
            pl.BlockSpec((2 * FOURIER_GROUP_DIM, FOURIER_GROUP_DIM), lambda k: (0, 0)),
        ],
        out_specs=pl.BlockSpec((n2, kb * ncol), lambda k: (0, k)),
        out_shape=jax.ShapeDtypeStruct((n2, n1 * ncol), BF16),
        compiler_params=_params("parallel"),
        name="dft_stage2",
    )(z.reshape(2, n2, n1 * ncol), f2, cc)
    return y.reshape(s, ncol)


def _score_bound(gq, gk):
    bq = jnp.max(jnp.abs(gq), axis=-1, keepdims=True)
    bk = jnp.max(jnp.abs(gk), axis=-1, keepdims=True)
    return (QK_HEAD_DIM * (QK_HEAD_DIM ** -0.5) * LOG2_E) * bq * bk


def _for_each_q_tile(qt_ref, o_ref, tile_fn):
    tq = min(Q_TILE, qt_ref.shape[2])

    def sub(i, carry):
        off = pl.multiple_of(i * tq, tq)
        tile_fn(qt_ref[0, :, pl.ds(off, tq)], o_ref.at[pl.ds(off, tq), :])
        return carry

    lax.fori_loop(0, qt_ref.shape[2] // tq, sub, 0)


def _attn_prologue(qt):
    row = lax.broadcasted_iota(jnp.int32, qt.shape, 0)
    zero = jnp.zeros_like(qt)
    return jnp.where(row < QK_HEAD_DIM, qt, zero), jnp.where(row >= QK_HEAD_DIM, qt, zero)


def _attn_epilogue(o1, o2, lq1_ref, lk1_ref, lq2_ref, lk2_ref, gsub_ref, o_ref, lambda_init):
    lam = (jnp.exp(jnp.sum(lq1_ref[...] * lk1_ref[...], axis=-1, keepdims=True))
           - jnp.exp(jnp.sum(lq2_ref[...] * lk2_ref[...], axis=-1, keepdims=True)) + lambda_init)
    o = o1 - lam * o2
    ms = jnp.mean(o * o, axis=0, keepdims=True)
    y = o * lax.rsqrt(ms + SUBLN_EPS) * gsub_ref[...] * (1.0 - lambda_init)
    o_ref[...] = y.T.astype(BF16)


def _attn_online_body(qt_ref, k_ref, vt_ref, gq_ref, gk_ref, lq1_ref, lk1_ref, lq2_ref, lk2_ref, gsub_ref, o_ref,
                      m_ref, l_ref, acc_ref, *, lambda_init):
    del gq_ref, gk_ref
    nk, tk = vt_ref.shape[1], vt_ref.shape[3]

    def tile(qt, o_view):
        q_halves = _attn_prologue(qt)
        m_ref[...] = jnp.full(m_ref.shape, M_INIT, F32)
        l_ref[...] = jnp.zeros(l_ref.shape, F32)
        acc_ref[...] = jnp.zeros(acc_ref.shape, F32)

        def chunk(c, carry):
            off = pl.multiple_of(c * tk, tk)
            kc = k_ref[pl.ds(off, tk), :]
            vc = vt_ref[0, c]
            for hf in range(2):
                s = jnp.dot(kc, q_halves[hf], preferred_element_type=F32)
                m_old = m_ref[hf]
                m_new = jnp.maximum(m_old, jnp.max(s, axis=0, keepdims=True))
                alpha = jnp.exp2(m_old - m_new)
                p = jnp.exp2(s - m_new)
                l_ref[hf] = alpha * l_ref[hf] + jnp.sum(p, axis=0, keepdims=True)
                acc_ref[hf] = alpha * acc_ref[hf] + jnp.dot(vc, p.astype(BF16), preferred_element_type=F32)
                m_ref[hf] = m_new
            return carry

        lax.fori_loop(0, nk, chunk, 0)
        _attn_epilogue(acc_ref[0] / l_ref[0], acc_ref[1] / l_ref[1],
                       lq1_ref, lk1_ref, lq2_ref, lk2_ref, gsub_ref, o_view, lambda_init)

    _for_each_q_tile(qt_ref, o_ref, tile)


def _attn_bounded_body(qt_ref, k_ref, vt_ref, gq_ref, gk_ref, lq1_ref, lk1_ref, lq2_ref, lk2_ref, gsub_ref, o_ref,
                       l_ref, acc_ref, *, lambda_init):
    nk, tk = vt_ref.shape[1], vt_ref.shape[3]
    bound = _score_bound(gq_ref[...], gk_ref[...])

    per_chunk = tk // KV_UNIT
    unroll = math.gcd(nk, KV_UNROLL)
    units = [(cc, j, hf) for cc in range(unroll) for j in range(per_chunk) for hf in range(2)]

    tq = min(Q_TILE, qt_ref.shape[2])
    n_tiles = qt_ref.shape[2] // tq

    def q_rows(i):
        return pl.ds(i * tq if isinstance(i, int) else pl.multiple_of(i * tq, tq), tq)

    def accumulate(i):
        q_halves = _attn_prologue(qt_ref[0, :, q_rows(i)])

        def trip(t, first_trip):
            def scores(cc, j, hf):
                off = pl.multiple_of((t * unroll + cc) * tk + j * KV_UNIT, KV_UNIT)
                return jnp.dot(k_ref[pl.ds(off, KV_UNIT), :], q_halves[hf], preferred_element_type=F32)

            pending = [scores(*u) for u in units[:QK_LOOKAHEAD]]
            for n, (cc, j, hf) in enumerate(units):
                s = pending.pop(0)
                if n + QK_LOOKAHEAD < len(units):
                    pending.append(scores(*units[n + QK_LOOKAHEAD]))
                p = jnp.exp2(s - bound)
                psum = jnp.sum(p.reshape(KV_UNIT // 8, 8, tq), axis=0)
                vc = vt_ref[0, t * unroll + cc, :, j * KV_UNIT:(j + 1) * KV_UNIT]
                pv = jnp.dot(vc, p.astype(BF16), preferred_element_type=F32)
                if first_trip and n < 2:
                    l_ref[hf] = psum
                    acc_ref[hf] = pv
                else:
                    l_ref[hf] += psum
                    acc_ref[hf] += pv

        trip(0, True)
        if nk // unroll > 1:
            lax.fori_loop(1, nk // unroll, lambda t, c: (trip(t, False), c)[1], 0)

    def finish(i):
        l1 = jnp.sum(l_ref[0], axis=0, keepdims=True)
        l2 = jnp.sum(l_ref[1], axis=0, keepdims=True)
        _attn_epilogue(acc_ref[0] / l1, acc_ref[1] / l2, lq1_ref, lk1_ref, lq2_ref, lk2_ref, gsub_ref,
                       o_ref.at[q_rows(i), :], lambda_init)

    accumulate(0)

    def step(i, carry):
        finish(i - 1)
        accumulate(i)
        return carry

    lax.fori_loop(1, n_tiles, step, 0)
    finish(n_tiles - 1)


def _diff_attention(qt, k, vt, gq, gk, lq1, lk1, lq2, lk2, gsub, lambda_init, online):
    h, _, s = qt.shape
    nk, tk = vt.shape[1], vt.shape[3]
    tq = min(Q_TILE, s)
    tqb = min(tq * Q_TILES_PER_STEP, s)
    vec = pl.BlockSpec((1, QK_HEAD_DIM), lambda hh, i: (0, 0))
    if online:
        body = _attn_online_body
        scratch = [pltpu.VMEM((2, 1, tq), F32), pltpu.VMEM((2, 1, tq), F32), pltpu.VMEM((2, V_HEAD_DIM, tq), F32)]
    else:
        body = _attn_bounded_body
        scratch = [pltpu.VMEM((2, 8, tq), F32), pltpu.VMEM((2, V_HEAD_DIM, tq), F32)]
    return pl.pallas_call(
        functools.partial(body, lambda_init=lambda_init),
        grid=(h, s // tqb),
        in_specs=[
            pl.BlockSpec((1, V_HEAD_DIM, tqb), lambda hh, i: (hh, 0, i)),
            pl.BlockSpec((s, V_HEAD_DIM), lambda hh, i: (0, hh)),
            pl.BlockSpec((1, nk, V_HEAD_DIM, tk), lambda hh, i: (hh, 0, 0, 0)),
            vec, vec, vec, vec, vec, vec,
            pl.BlockSpec((V_HEAD_DIM, 1), lambda hh, i: (0, 0)),
        ],
        out_specs=pl.BlockSpec((tqb, V_HEAD_DIM), lambda hh, i: (i, hh)),
        out_shape=jax.ShapeDtypeStruct((s, h * V_HEAD_DIM), BF16),
        scratch_shapes=scratch,
        compiler_params=_params("parallel", "parallel"),
        name="diff_attention_online" if online else "diff_attention_bounded",
    )(qt, k, vt, gq, gk, lq1, lk1, lq2, lk2, gsub)


def _outproj_body(x_ref, yf_ref, ya_ref, wf_ref, wa_ref, o_ref):
    o_ref[...] = (x_ref[...]
                  + jnp.dot(yf_ref[...], wf_ref[...], preferred_element_type=F32)
                  + jnp.dot(ya_ref[...], wa_ref[...], preferred_element_type=F32))


def _outproj(x, yf, ya, w_out, layer):
    s, d = x.shape
    nf, na = yf.shape[1], ya.shape[1]
    assert nf == na
    tm = min(ROW_TILE, s)
    return pl.pallas_call(
        _outproj_body,
        grid=(s // tm,),
        in_specs=[
            pl.BlockSpec((tm, d), lambda i: (i, 0)),
            pl.BlockSpec((tm, nf), lambda i: (i, 0)),
            pl.BlockSpec((tm, na), lambda i: (i, 0)),
            pl.BlockSpec((None, nf, d), lambda i: (layer, 0, 0)),
            pl.BlockSpec((None, na, d), lambda i: (layer, 1, 0)),
        ],
        out_specs=pl.BlockSpec((tm, d), lambda i: (i, 0)),
        out_shape=jax.ShapeDtypeStruct((s, d), F32),
        compiler_params=_params("parallel"),
        name="out_proj",
    )(x, yf, ya, w_out, w_out)


def kernel(x, positions, norm_ffn1, w1_gate, w1_up, w1_down, norm_mix, w_in, q_norm, k_norm,
           lambda_q1, lambda_k1, lambda_q2, lambda_k2, subln, w_out, norm_ffn2, w2_gate, w2_up, w2_down):
    b, s, d = x.shape
    depth = w_in.shape[0]
    assert b == 1 and s % DFT_S2 == 0
    xs = x.reshape(s, d)
    win, wout = w_in.astype(BF16), w_out.astype(BF16)

    cos, sina, sinb = _rope_tables(positions.reshape(s))
    gmat = _group_mean_matrix()
    tables = _dft_tables(s)
    q_scale = (QK_HEAD_DIM ** -0.5) * LOG2_E

    for l in range(depth):
        lambda_init = 0.8 - 0.6 * math.exp(-0.3 * l)
        xs = _ffn(xs, norm_ffn1[l].reshape(1, d), w1_gate, w1_up, w1_down, l)

        g_mix = norm_mix[l].reshape(1, d)
        gq = jnp.tile(q_norm[l], 2).reshape(1, V_HEAD_DIM)
        gk = jnp.tile(k_norm[l], 2).reshape(1, V_HEAD_DIM)
        u_f, qt, kk, vt = _in_proj(xs, g_mix, win, l, gq, gk, gmat, cos, sina, sinb, q_scale)
        y_f = _fourier_mix(u_f, tables)
        gq64, gk64 = q_norm[l].reshape(1, QK_HEAD_DIM), k_norm[l].reshape(1, QK_HEAD_DIM)
        attn_args = (qt, kk, vt, gq64, gk64,
                     lambda_q1[l].reshape(1, -1), lambda_k1[l].reshape(1, -1),
                     lambda_q2[l].reshape(1, -1), lambda_k2[l].reshape(1, -1),
                     subln[l].reshape(V_HEAD_DIM, 1))
        y_a = lax.cond(
            _score_bound(gq64, gk64)[0, 0] <= MAX_BOUNDED_SOFTMAX_LOG2,
            lambda *a: _diff_attention(*a, lambda_init, False),
            lambda *a: _diff_attention(*a, lambda_init, True),
            *attn_args)
        xs = _outproj(xs, y_f, y_a, wout, l)

        xs = _ffn(xs, norm_ffn2[l].reshape(1, d), w2_gate, w2_up, w2_down, l)
    return xs.reshape(b, s, d)
```

```python
import functools
import math

import numpy as np
import jax
import jax.numpy as jnp
from jax import lax
from jax.experimental import pallas as pl
from jax.experimental.pallas import tpu as pltpu

F32 = jnp.float32
BF16 = jnp.bfloat16

N_HEADS = 8
V_HEAD_DIM = 128
QK_HEAD_DIM = 64
N_FOURIER_GROUPS = 8
FOURIER_GROUP_DIM = 128
ROPE_THETA = 10000.0
NORM_EPS = 1e-6
SUBLN_EPS = 1e-5
LOG2_E = 1.4426950408889634

V7X_VMEM_BYTES = 64 * 1024 * 1024
VMEM_LIMIT_BYTES = V7X_VMEM_BYTES - 4 * 1024 * 1024
LANES = 128

ROW_TILE = 512
FFN_ROW_TILE = 1024
FF_TILE = 512
Q_TILE = 512
Q_TILES_PER_STEP = 8
KV_TILE = 512
DFT_S2 = 128
F1_S2_TILE = 8
F2_K1_PER_STEP = 4
KV_UNROLL = 16
KV_UNIT = 256
QK_LOOKAHEAD = 2
M_INIT = -1e30
MAX_BOUNDED_SOFTMAX_LOG2 = 60.0


def _params(*semantics):
    return pltpu.CompilerParams(dimension_semantics=semantics, vmem_limit_bytes=VMEM_LIMIT_BYTES)


def _rms_rows(x, g):
    ms = jnp.mean(x * x, axis=-1, keepdims=True)
    return x * lax.rsqrt(ms + NORM_EPS) * g


def _ffn_body(x_hbm, g_ref, wg_hbm, wu_hbm, wd_hbm, o_ref, xrow_ref, xn_ref, wg_buf, wu_buf, wd_buf, x_sem, w_sem,
              *, layer, tf):
    i, n_i = pl.program_id(0), pl.num_programs(0)
    tm = o_ref.shape[0]
    nf = wg_hbm.shape[2] // tf

    def x_tile_copy(tile):
        return pltpu.make_async_copy(x_hbm.at[pl.ds(tile * tm, tm), :], xrow_ref, x_sem)

    def weight_copies(chunk, slot):
        cols = pl.ds(pl.multiple_of(chunk * tf, tf), tf)
        return (pltpu.make_async_copy(wg_hbm.at[layer, :, cols], wg_buf.at[slot], w_sem.at[0, slot]),
                pltpu.make_async_copy(wu_hbm.at[layer, :, cols], wu_buf.at[slot], w_sem.at[1, slot]),
                pltpu.make_async_copy(wd_hbm.at[layer, cols, :], wd_buf.at[slot], w_sem.at[2, slot]))

    def start_weights(chunk, slot):
        for n, cp in enumerate(weight_copies(chunk, slot)):
            cp.start(priority=n % 2)

    @pl.when(i == 0)
    def _():
        x_tile_copy(0).start()
        start_weights(0, 0)

    x_tile_copy(i).wait()
    x = xrow_ref[...]
    xn_ref[...] = _rms_rows(x, g_ref[...]).astype(BF16)
    o_ref[...] = x

    @pl.when(i + 1 < n_i)
    def _():
        x_tile_copy(i + 1).start()

    def chunk(c, carry):
        step = i * nf + c
        slot = lax.rem(step, 2)
        for cp in weight_copies(c, slot):
            cp.wait()

        @pl.when(step + 1 < n_i * nf)
        def _():
            start_weights(jnp.where(c + 1 == nf, 0, c + 1), 1 - slot)

        xn = xn_ref[...]
        h = jnp.dot(xn, wg_buf[slot].astype(BF16), preferred_element_type=F32)
        u = jnp.dot(xn, wu_buf[slot].astype(BF16), preferred_element_type=F32)
        a = (h * jax.nn.sigmoid(h) * u).astype(BF16)
        o_ref[...] += 0.5 * jnp.dot(a, wd_buf[slot].astype(BF16), preferred_element_type=F32)
        return carry

    lax.fori_loop(0, nf, chunk, 0)


def _ffn(x, g, wg, wu, wd, layer):
    s, d = x.shape
    f = wg.shape[-1]
    tm, tf = min(FFN_ROW_TILE, s), min(FF_TILE, f)
    assert s % tm == 0 and f % tf == 0
    hbm = pl.BlockSpec(memory_space=pl.ANY)
    return pl.pallas_call(
        functools.partial(_ffn_body, layer=layer, tf=tf),
        grid=(s // tm,),
        in_specs=[hbm, pl.BlockSpec((1, d), lambda i: (0, 0)), hbm, hbm, hbm],
        out_specs=pl.BlockSpec((tm, d), lambda i: (i, 0)),
        out_shape=jax.ShapeDtypeStruct((s, d), F32),
        scratch_shapes=[
            pltpu.VMEM((tm, d), F32),
            pltpu.VMEM((tm, d), BF16),
            pltpu.VMEM((2, d, tf), F32),
            pltpu.VMEM((2, d, tf), F32),
            pltpu.VMEM((2, tf, d), F32),
            pltpu.SemaphoreType.DMA(()),
            pltpu.SemaphoreType.DMA((3, 2)),
        ],
        compiler_params=_params("arbitrary"),
        name="ffn",
    )(x, g, wg, wu, wd)


def _rope_body(pos_ref, invf_ref, cos_ref, sina_ref, sinb_ref):
    ang = pos_ref[...] * invf_ref[...]
    c = jnp.cos(ang)
    s = jnp.sin(ang)
    lane = lax.broadcasted_iota(jnp.int32, ang.shape, 1)
    first = (lane & (QK_HEAD_DIM // 2)) == 0
    cos_ref[...] = c
    sina_ref[...] = jnp.where(first, -s, 0.0)
    sinb_ref[...] = jnp.where(first, 0.0, s)


def _rope_tables(positions):
    s = positions.shape[0]
    inv_freq = 1.0 / (ROPE_THETA ** (jnp.arange(0, QK_HEAD_DIM, 2, dtype=F32) / QK_HEAD_DIM))
    invf = jnp.tile(inv_freq, 2 * V_HEAD_DIM // QK_HEAD_DIM).reshape(1, V_HEAD_DIM)
    pos = positions.astype(F32).reshape(s, 1)
    ts = min(1024, s)
    tab = jax.ShapeDtypeStruct((s, V_HEAD_DIM), F32)
    return pl.pallas_call(
        _rope_body,
        grid=(s // ts,),
        in_specs=[pl.BlockSpec((ts, 1), lambda i: (i, 0)), pl.BlockSpec((1, V_HEAD_DIM), lambda i: (0, 0))],
        out_specs=[pl.BlockSpec((ts, V_HEAD_DIM), lambda i: (i, 0))] * 3,
        out_shape=[tab, tab, tab],
        compiler_params=_params("parallel"),
        name="rope_tables",
    )(pos, invf)


def _group_mean_matrix():
    idx = np.arange(V_HEAD_DIM) // QK_HEAD_DIM
    return jnp.asarray((idx[:, None] == idx[None, :]).astype(np.float32) / QK_HEAD_DIM, dtype=BF16)


def _qk_norm_rope(t, gain, gmat, cos, sina, sinb):
    sq = t * t
    hi = sq.astype(BF16)
    lo = (sq - hi.astype(F32)).astype(BF16)
    ms = jnp.dot(hi, gmat, preferred_element_type=F32) + jnp.dot(lo, gmat, preferred_element_type=F32)
    tn = t * lax.rsqrt(ms + NORM_EPS) * gain
    half = QK_HEAD_DIM // 2
    return tn * cos + pltpu.roll(tn, V_HEAD_DIM - half, 1) * sina + pltpu.roll(tn, half, 1) * sinb


def _in_proj_body(x_ref, g_ref, w_ref, gq_ref, gk_ref, gmat_ref, cos_ref, sina_ref, sinb_ref,
                  u_ref, qt_ref, k_ref, vt_ref, *, q_scale):
    n = N_HEADS * V_HEAD_DIM
    xn = _rms_rows(x_ref[...], g_ref[...]).astype(BF16)

    def group(c):
        return jnp.dot(xn, w_ref[:, c * n:(c + 1) * n], preferred_element_type=F32)

    u_ref[...] = group(0).astype(BF16)
    gmat, cos, sina, sinb = gmat_ref[...], cos_ref[...], sina_ref[...], sinb_ref[...]
    pq = group(1)
    for h in range(N_HEADS):
        t = _qk_norm_rope(pq[:, h * V_HEAD_DIM:(h + 1) * V_HEAD_DIM], gq_ref[...], gmat, cos, sina, sinb)
        qt_ref[h] = (t * q_scale).T.astype(BF16)
    pk = group(2)
    for h in range(N_HEADS):
        sl = slice(h * V_HEAD_DIM, (h + 1) * V_HEAD_DIM)
        k_ref[:, sl] = _qk_norm_rope(pk[:, sl], gk_ref[...], gmat, cos, sina, sinb).astype(BF16)
    pv = group(3)
    for h in range(N_HEADS):
        vt_ref[h, 0] = pv[:, h * V_HEAD_DIM:(h + 1) * V_HEAD_DIM].T.astype(BF16)


def _in_proj(x, g, w_in, layer, gq, gk, gmat, cos, sina, sinb, q_scale):
    s, d = x.shape
    n = N_HEADS * V_HEAD_DIM
    tm = min(ROW_TILE, s)
    assert tm == min(KV_TILE, s) and w_in.shape[-1] == 4 * n
    row_tab = pl.BlockSpec((tm, V_HEAD_DIM), lambda i: (i, 0))
    head_vec = pl.BlockSpec((1, V_HEAD_DIM), lambda i: (0, 0))
    return pl.pallas_call(
        functools.partial(_in_proj_body, q_scale=q_scale),
        grid=(s // tm,),
        in_specs=[
            pl.BlockSpec((tm, d), lambda i: (i, 0)),
            pl.BlockSpec((1, d), lambda i: (0, 0)),
            pl.BlockSpec((None, d, 4 * n), lambda i: (layer, 0, 0)),
            head_vec, head_vec,
            pl.BlockSpec((V_HEAD_DIM, V_HEAD_DIM), lambda i: (0, 0)),
            row_tab, row_tab, row_tab,
        ],
        out_specs=[
            pl.BlockSpec((tm, n), lambda i: (i, 0)),
            pl.BlockSpec((N_HEADS, V_HEAD_DIM, tm), lambda i: (0, 0, i)),
            pl.BlockSpec((tm, n), lambda i: (i, 0)),
            pl.BlockSpec((N_HEADS, 1, V_HEAD_DIM, tm), lambda i: (0, i, 0, 0)),
        ],
        out_shape=[
            jax.ShapeDtypeStruct((s, n), BF16),
            jax.ShapeDtypeStruct((N_HEADS, V_HEAD_DIM, s), BF16),
            jax.ShapeDtypeStruct((s, n), BF16),
            jax.ShapeDtypeStruct((N_HEADS, s // tm, V_HEAD_DIM, tm), BF16),
        ],
        compiler_params=_params("parallel"),
        name="in_proj",
    )(x, g, w_in, gq, gk, gmat, cos, sina, sinb)


def _dft_tables(s):
    n1, n2 = s // DFT_S2, DFT_S2
    a1 = 2.0 * np.pi * np.outer(np.arange(n1), np.arange(n1)) / n1
    cs1 = np.concatenate([np.cos(a1), -np.sin(a1)], axis=0)
    at = 2.0 * np.pi * np.outer(np.arange(n1), np.arange(n2)) / s
    nt = n2 // F1_S2_TILE
    twc = np.cos(at).reshape(n1, nt, F1_S2_TILE).transpose(1, 0, 2)
    tws = np.sin(at).reshape(n1, nt, F1_S2_TILE).transpose(1, 0, 2)
    a2 = 2.0 * np.pi * np.outer(np.arange(n2), np.arange(n2)) / n2
    c2, s2 = np.cos(a2), np.sin(a2)
    f2 = np.block([[c2, s2], [-s2, c2]])
    ac = 2.0 * np.pi * np.outer(np.arange(FOURIER_GROUP_DIM), np.arange(FOURIER_GROUP_DIM)) / FOURIER_GROUP_DIM
    cc = np.concatenate([np.cos(ac), np.sin(ac)], axis=0)
    return (jnp.asarray(cs1, dtype=BF16), jnp.asarray(twc, dtype=F32), jnp.asarray(tws, dtype=F32),
            jnp.asarray(f2, dtype=BF16), jnp.asarray(cc, dtype=BF16))


def _f1_body(u_ref, cs_ref, twc_ref, tws_ref, z_ref, *, n1, ncol):
    zc = jnp.dot(cs_ref[...], u_ref[...], preferred_element_type=F32)
    twc = twc_ref[0]
    tws = tws_ref[0]
    for j in range(F1_S2_TILE):
        zr = zc[0:n1, j * ncol:(j + 1) * ncol]
        zi = zc[n1:2 * n1, j * ncol:(j + 1) * ncol]
        c = twc[:, j:j + 1]
        s = tws[:, j:j + 1]
        z_ref[0, j] = (zr * c + zi * s).astype(BF16)
        z_ref[1, j] = (zi * c - zr * s).astype(BF16)


def _f2_body(z_ref, f_ref, cc_ref, y_ref, *, norm):
    n2 = DFT_S2
    gd = FOURIER_GROUP_DIM
    zz = z_ref[...].reshape(2 * n2, z_ref.shape[-1])
    xb = jnp.dot(f_ref[...], zz, preferred_element_type=F32).astype(BF16)
    for g in range(z_ref.shape[-1] // gd):
        lhs = jnp.concatenate([xb[0:n2, g * gd:(g + 1) * gd], xb[n2:2 * n2, g * gd:(g + 1) * gd]], axis=1)
        y = jnp.dot(lhs, cc_ref[...], preferred_element_type=F32)
        y_ref[:, g * gd:(g + 1) * gd] = (y * norm).astype(BF16)


def _fourier_mix(u, tables):
    s, ncol = u.shape
    n1, n2 = s // DFT_S2, DFT_S2
    cs1, twc, tws, f2, cc = tables
    ts2 = F1_S2_TILE
    z = pl.pallas_call(
        functools.partial(_f1_body, n1=n1, ncol=ncol),
        grid=(n2 // ts2,),
        in_specs=[
            pl.BlockSpec((n1, ts2 * ncol), lambda t: (0, t)),
            pl.BlockSpec((2 * n1, n1), lambda t: (0, 0)),
            pl.BlockSpec((1, n1, ts2), lambda t: (t, 0, 0)),
            pl.BlockSpec((1, n1, ts2), lambda t: (t, 0, 0)),
        ],
        out_specs=pl.BlockSpec((2, ts2, n1, ncol), lambda t: (0, t, 0, 0)),
        out_shape=jax.ShapeDtypeStruct((2, n2, n1, ncol), BF16),
        compiler_params=_params("parallel"),
        name="dft_stage1",
    )(u.reshape(n1, n2 * ncol), cs1, twc, tws)
    norm = 1.0 / math.sqrt(s * FOURIER_GROUP_DIM)
    kb = math.gcd(n1, F2_K1_PER_STEP)
    y = pl.pallas_call(
        functools.partial(_f2_body, norm=norm),
        grid=(n1 // kb,),
        in_specs=[
            pl.BlockSpec((2, n2, kb * ncol), lambda k: (0, 0, k)),
            pl.BlockSpec((2 * n2, 2 * n2), lambda k: (0, 0)),
            pl.BlockSpec((2 * FOURIER_GROUP_DIM, FOURIER_GROUP_DIM), lambda k: (0, 0)),
        ],
        out_specs=pl.BlockSpec((n2, kb * ncol), lambda k: (0, k)),
        out_shape=jax.ShapeDtypeStruct((n2, n1 * ncol), BF16),
        compiler_params=_params("parallel"),
        name="dft_stage2",
    )(z.reshape(2, n2, n1 * ncol), f2, cc)
    return y.reshape(s, ncol)


def _score_bound(gq, gk):
    bq = jnp.max(jnp.abs(gq), axis=-1, keepdims=True)
    bk = jnp.max(jnp.abs(gk), axis=-1, keepdims=True)
    return (QK_HEAD_DIM * (QK_HEAD_DIM ** -0.5) * LOG2_E) * bq * bk


def _for_each_q_tile(qt_ref, o_ref, tile_fn):
    tq = min(Q_TILE, qt_ref.shape[2])

    def sub(i, carry):
        off = pl.multiple_of(i * tq, tq)
        tile_fn(qt_ref[0, :, pl.ds(off, tq)], o_ref.at[pl.ds(off, tq), :])
        return carry

    lax.fori_loop(0, qt_ref.shape[2] // tq, sub, 0)


def _attn_prologue(qt):
    row = lax.broadcasted_iota(jnp.int32, qt.shape, 0)
    zero = jnp.zeros_like(qt)
    return jnp.where(row < QK_HEAD_DIM, qt, zero), jnp.where(row >= QK_HEAD_DIM, qt, zero)


def _attn_epilogue(o1, o2, lq1_ref, lk1_ref, lq2_ref, lk2_ref, gsub_ref, o_ref, lambda_init):
    lam = (jnp.exp(jnp.sum(lq1_ref[...] * lk1_ref[...], axis=-1, keepdims=True))
           - jnp.exp(jnp.sum(lq2_ref[...] * lk2_ref[...], axis=-1, keepdims=True)) + lambda_init)
    o = o1 - lam * o2
    ms = jnp.mean(o * o, axis=0, keepdims=True)
    y = o * lax.rsqrt(ms + SUBLN_EPS) * gsub_ref[...] * (1.0 - lambda_init)
    o_ref[...] = y.T.astype(BF16)


def _attn_online_body(qt_ref, k_ref, vt_ref, gq_ref, gk_ref, lq1_ref, lk1_ref, lq2_ref, lk2_ref, gsub_ref, o_ref,
                      m_ref, l_ref, acc_ref, *, lambda_init):
    del gq_ref, gk_ref
    nk, tk = vt_ref.shape[1], vt_ref.shape[3]

    def tile(qt, o_view):
        q_halves = _attn_prologue(qt)
        m_ref[...] = jnp.full(m_ref.shape, M_INIT, F32)
        l_ref[...] = jnp.zeros(l_ref.shape, F32)
        acc_ref[...] = jnp.zeros(acc_ref.shape, F32)

        def chunk(c, carry):
            off = pl.multiple_of(c * tk, tk)
            kc = k_ref[pl.ds(off, tk), :]
            vc = vt_ref[0, c]
            for hf in range(2):
                s = jnp.dot(kc, q_halves[hf], preferred_element_type=F32)
                m_old = m_ref[hf]
                m_new = jnp.maximum(m_old, jnp.max(s, axis=0, keepdims=True))
                alpha = jnp.exp2(m_old - m_new)
                p = jnp.exp2(s - m_new)
                l_ref[hf] = alpha * l_ref[hf] + jnp.sum(p, axis=0, keepdims=True)
                acc_ref[hf] = alpha * acc_ref[hf] + jnp.dot(vc, p.astype(BF16), preferred_element_type=F32)
                m_ref[hf] = m_new
            return carry

        lax.fori_loop(0, nk, chunk, 0)
        _attn_epilogue(acc_ref[0] / l_ref[0], acc_ref[1] / l_ref[1],
                       lq1_ref, lk1_ref, lq2_ref, lk2_ref, gsub_ref, o_view, lambda_init)

    _for_each_q_tile(qt_ref, o_ref, tile)


def _attn_bounded_body(qt_ref, k_ref, vt_ref, gq_ref, gk_ref, lq1_ref, lk1_ref, lq2_ref, lk2_ref, gsub_ref, o_ref,
                       l_ref, acc_ref, *, lambda_init):
    nk, tk = vt_ref.shape[1], vt_ref.shape[3]
    bound = _score_bound(gq_ref[...], gk_ref[...])

    per_chunk = tk // KV_UNIT
    unroll = math.gcd(nk, KV_UNROLL)
    units = [(cc, j, hf) for cc in range(unroll) for j in range(per_chunk) for hf in range(2)]

    tq = min(Q_TILE, qt_ref.shape[2])
    n_tiles = qt_ref.shape[2] // tq

    def q_rows(i):
        return pl.ds(i * tq if isinstance(i, int) else pl.multiple_of(i * tq, tq), tq)

    def accumulate(i):
        q_halves = _attn_prologue(qt_ref[0, :, q_rows(i)])

        def trip(t, first_trip):
            def scores(cc, j, hf):
                off = pl.multiple_of((t * unroll + cc) * tk + j * KV_UNIT, KV_UNIT)
                return jnp.dot(k_ref[pl.ds(off, KV_UNIT), :], q_halves[hf], preferred_element_type=F32)

            pending = [scores(*u) for u in units[:QK_LOOKAHEAD]]
            for n, (cc, j, hf) in enumerate(units):
                s = pending.pop(0)
                if n + QK_LOOKAHEAD < len(units):
                    pending.append(scores(*units[n + QK_LOOKAHEAD]))
                p = jnp.exp2(s - bound)
                psum = jnp.sum(p.reshape(KV_UNIT // 8, 8, tq), axis=0)
                vc = vt_ref[0, t * unroll + cc, :, j * KV_UNIT:(j + 1) * KV_UNIT]
                pv = jnp.dot(vc, p.astype(BF16), preferred_element_type=F32)
                if first_trip and n < 2:
                    l_ref[hf] = psum
                    acc_ref[hf] = pv
                else:
                    l_ref[hf] += psum
                    acc_ref[hf] += pv

        trip(0, True)
        if nk // unroll > 1:
            lax.fori_loop(1, nk // unroll, lambda t, c: (trip(t, False), c)[1], 0)

    def finish(i):
        l1 = jnp.sum(l_ref[0], axis=0, keepdims=True)
        l2 = jnp.sum(l_ref[1], axis=0, keepdims=True)
        _attn_epilogue(acc_ref[0] / l1, acc_ref[1] / l2, lq1_ref, lk1_ref, lq2_ref, lk2_ref, gsub_ref,
                       o_ref.at[q_rows(i), :], lambda_init)

    accumulate(0)

    def step(i, carry):
        finish(i - 1)
        accumulate(i)
        return carry

    lax.fori_loop(1, n_tiles, step, 0)
    finish(n_tiles - 1)


def _diff_attention(qt, k, vt, gq, gk, lq1, lk1, lq2, lk2, gsub, lambda_init, online):
    h, _, s = qt.shape
    nk, tk = vt.shape[1], vt.shape[3]
    tq = min(Q_TILE, s)
    tqb = min(tq * Q_TILES_PER_STEP, s)
    vec = pl.BlockSpec((1, QK_HEAD_DIM), lambda hh, i: (0, 0))
    if online:
        body = _attn_online_body
        scratch = [pltpu.VMEM((2, 1, tq), F32), pltpu.VMEM((2, 1, tq), F32), pltpu.VMEM((2, V_HEAD_DIM, tq), F32)]
    else:
        body = _attn_bounded_body
        scratch = [pltpu.VMEM((2, 8, tq), F32), pltpu.VMEM((2, V_HEAD_DIM, tq), F32)]
    return pl.pallas_call(
        functools.partial(body, lambda_init=lambda_init),
        grid=(h, s // tqb),
        in_specs=[
            pl.BlockSpec((1, V_HEAD_DIM, tqb), lambda hh, i: (hh, 0, i)),
            pl.BlockSpec((s, V_HEAD_DIM), lambda hh, i: (0, hh)),
            pl.BlockSpec((1, nk, V_HEAD_DIM, tk), lambda hh, i: (hh, 0, 0, 0)),
            vec, vec, vec, vec, vec, vec,
            pl.BlockSpec((V_HEAD_DIM, 1), lambda hh, i: (0, 0)),
        ],
        out_specs=pl.BlockSpec((tqb, V_HEAD_DIM), lambda hh, i: (i, hh)),
        out_shape=jax.ShapeDtypeStruct((s, h * V_HEAD_DIM), BF16),
        scratch_shapes=scratch,
        compiler_params=_params("parallel", "parallel"),
        name="diff_attention_online" if online else "diff_attention_bounded",
    )(qt, k, vt, gq, gk, lq1, lk1, lq2, lk2, gsub)


def _outproj_body(x_ref, yf_ref, ya_ref, wf_ref, wa_ref, o_ref):
    o_ref[...] = (x_ref[...]
                  + jnp.dot(yf_ref[...], wf_ref[...], preferred_element_type=F32)
                  + jnp.dot(ya_ref[...], wa_ref[...], preferred_element_type=F32))


def _outproj(x, yf, ya, w_out, layer):
    s, d = x.shape
    nf, na = yf.shape[1], ya.shape[1]
    assert nf == na
    tm = min(ROW_TILE, s)
    return pl.pallas_call(
        _outproj_body,
        grid=(s // tm,),
        in_specs=[
            pl.BlockSpec((tm, d), lambda i: (i, 0)),
            pl.BlockSpec((tm, nf), lambda i: (i, 0)),
            pl.BlockSpec((tm, na), lambda i: (i, 0)),
            pl.BlockSpec((None, nf, d), lambda i: (layer, 0, 0)),
            pl.BlockSpec((None, na, d), lambda i: (layer, 1, 0)),
        ],
        out_specs=pl.BlockSpec((tm, d), lambda i: (i, 0)),
        out_shape=jax.ShapeDtypeStruct((s, d), F32),
        compiler_params=_params("parallel"),
        name="out_proj",
    )(x, yf, ya, w_out, w_out)


def kernel(x, positions, norm_ffn1, w1_gate, w1_up, w1_down, norm_mix, w_in, q_norm, k_norm,
           lambda_q1, lambda_k1, lambda_q2, lambda_k2, subln, w_out, norm_ffn2, w2_gate, w2_up, w2_down):
    b, s, d = x.shape
    depth = w_in.shape[0]
    assert b == 1 and s % DFT_S2 == 0
    xs = x.reshape(s, d)
    win, wout = w_in.astype(BF16), w_out.astype(BF16)

    cos, sina, sinb = _rope_tables(positions.reshape(s))
    gmat = _group_mean_matrix()
    tables = _dft_tables(s)
    q_scale = (QK_HEAD_DIM ** -0.5) * LOG2_E

    for l in range(depth):
        lambda_init = 0.8 - 0.6 * math.exp(-0.3 * l)
        xs = _ffn(xs, norm_ffn1[l].reshape(1, d), w1_gate, w1_up, w1_down, l)

        g_mix = norm_mix[l].reshape(1, d)
        gq = jnp.tile(q_norm[l], 2).reshape(1, V_HEAD_DIM)
        gk = jnp.tile(k_norm[l], 2).reshape(1, V_HEAD_DIM)
        u_f, qt, kk, vt = _in_proj(xs, g_mix, win, l, gq, gk, gmat, cos, sina, sinb, q_scale)
        y_f = _fourier_mix(u_f, tables)
        gq64, gk64 = q_norm[l].reshape(1, QK_HEAD_DIM), k_norm[l].reshape(1, QK_HEAD_DIM)
        attn_args = (qt, kk, vt, gq64, gk64,
                     lambda_q1[l].reshape(1, -1), lambda_k1[l].reshape(1, -1),
                     lambda_q2[l].reshape(1, -1), lambda_k2[l].reshape(1, -1),
                     subln[l].reshape(V_HEAD_DIM, 1))
        y_a = lax.cond(
            _score_bound(gq64, gk64)[0, 0] <= MAX_BOUNDED_SOFTMAX_LOG2,
            lambda *a: _diff_attention(*a, lambda_init, False),
            lambda *a: _diff_attention(*a, lambda_init, True),
            *attn_args)
        xs = _outproj(xs, y_f, y_a, wout, l)

        xs = _ffn(xs, norm_ffn2[l].reshape(1, d), w2_gate, w2_up, w2_down, l)
    return xs.reshape(b, s, d)
```

```python
import functools
import math

import numpy as np
import jax
import jax.numpy as jnp
from jax import lax
from jax.experimental import pallas as pl
from jax.experimental.pallas import tpu as pltpu

F32 = jnp.float32
BF16 = jnp.bfloat16

N_HEADS = 8
V_HEAD_DIM = 128
QK_HEAD_DIM = 64
N_FOURIER_GROUPS = 8
FOURIER_GROUP_DIM = 128
ROPE_THETA = 10000.0
NORM_EPS = 1e-6
SUBLN_EPS = 1e-5
LOG2_E = 1.4426950408889634

V7X_VMEM_BYTES = 64 * 1024 * 1024
VMEM_LIMIT_BYTES = V7X_VMEM_BYTES - 4 * 1024 * 1024
LANES = 128

ROW_TILE = 512
FFN_ROW_TILE = 1024
FF_TILE = 512
Q_TILE = 512
Q_TILES_PER_STEP = 8
KV_TILE = 512
DFT_S2 = 128
F1_S2_TILE = 16
F2_K1_PER_STEP = 4
KV_UNROLL = 16
KV_UNIT = 256
QK_LOOKAHEAD = 2
M_INIT = -1e30
MAX_BOUNDED_SOFTMAX_LOG2 = 60.0


def _params(*semantics):
    return pltpu.CompilerParams(dimension_semantics=semantics, vmem_limit_bytes=VMEM_LIMIT_BYTES)


def _rms_rows(x, g):
    ms = jnp.mean(x * x, axis=-1, keepdims=True)
    return x * lax.rsqrt(ms + NORM_EPS) * g


def _ffn_body(x_hbm, g_ref, wg_ref, wu_ref, wd_ref, o_ref, xrow_ref, xn_ref, x_sem):
    i, j = pl.program_id(0), pl.program_id(1)
    tm = o_ref.shape[0]

    def x_tile_copy(tile):
        return pltpu.make_async_copy(x_hbm.at[pl.ds(tile * tm, tm), :], xrow_ref, x_sem)

    @pl.when(jnp.logical_and(i == 0, j == 0))
    def _():
        x_tile_copy(0).start()

    @pl.when(j == 0)
    def _():
        x_tile_copy(i).wait()
        x = xrow_ref[...]
        xn_ref[...] = _rms_rows(x, g_ref[...]).astype(BF16)
        o_ref[...] = x

        @pl.when(i + 1 < pl.num_programs(0))
        def _():
            x_tile_copy(i + 1).start()

    xn = xn_ref[...]
    h = jnp.dot(xn, wg_ref[...].astype(BF16), preferred_element_type=F32)
    u = jnp.dot(xn, wu_ref[...].astype(BF16), preferred_element_type=F32)
    a = (h * jax.nn.sigmoid(h) * u).astype(BF16)
    o_ref[...] += 0.5 * jnp.dot(a, wd_ref[...].astype(BF16), preferred_element_type=F32)


def _ffn(x, g, wg, wu, wd, layer):
    s, d = x.shape
    f = wg.shape[-1]
    tm, tf = min(FFN_ROW_TILE, s), min(FF_TILE, f)
    return pl.pallas_call(
        _ffn_body,
        grid=(s // tm, f // tf),
        in_specs=[
            pl.BlockSpec(memory_space=pl.ANY),
            pl.BlockSpec((1, d), lambda i, j: (0, 0)),
            pl.BlockSpec((None, d, tf), lambda i, j: (layer, 0, j)),
            pl.BlockSpec((None, d, tf), lambda i, j: (layer, 0, j)),
            pl.BlockSpec((None, tf, d), lambda i, j: (layer, j, 0)),
        ],
        out_specs=pl.BlockSpec((tm, d), lambda i, j: (i, 0)),
        out_shape=jax.ShapeDtypeStruct((s, d), F32),
        scratch_shapes=[pltpu.VMEM((tm, d), F32), pltpu.VMEM((tm, d), BF16), pltpu.SemaphoreType.DMA(())],
        compiler_params=_params("arbitrary", "arbitrary"),
        name="ffn",
    )(x, g, wg, wu, wd)


def _rope_body(pos_ref, invf_ref, cos_ref, sina_ref, sinb_ref):
    ang = pos_ref[...] * invf_ref[...]
    c = jnp.cos(ang)
    s = jnp.sin(ang)
    lane = lax.broadcasted_iota(jnp.int32, ang.shape, 1)
    first = (lane & (QK_HEAD_DIM // 2)) == 0
    cos_ref[...] = c
    sina_ref[...] = jnp.where(first, -s, 0.0)
    sinb_ref[...] = jnp.where(first, 0.0, s)


def _rope_tables(positions):
    s = positions.shape[0]
    inv_freq = 1.0 / (ROPE_THETA ** (jnp.arange(0, QK_HEAD_DIM, 2, dtype=F32) / QK_HEAD_DIM))
    invf = jnp.tile(inv_freq, 2 * V_HEAD_DIM // QK_HEAD_DIM).reshape(1, V_HEAD_DIM)
    pos = positions.astype(F32).reshape(s, 1)
    ts = min(1024, s)
    tab = jax.ShapeDtypeStruct((s, V_HEAD_DIM), F32)
    return pl.pallas_call(
        _rope_body,
        grid=(s // ts,),
        in_specs=[pl.BlockSpec((ts, 1), lambda i: (i, 0)), pl.BlockSpec((1, V_HEAD_DIM), lambda i: (0, 0))],
        out_specs=[pl.BlockSpec((ts, V_HEAD_DIM), lambda i: (i, 0))] * 3,
        out_shape=[tab, tab, tab],
        compiler_params=_params("parallel"),
        name="rope_tables",
    )(pos, invf)


def _group_mean_matrix():
    idx = np.arange(V_HEAD_DIM) // QK_HEAD_DIM
    return jnp.asarray((idx[:, None] == idx[None, :]).astype(np.float32) / QK_HEAD_DIM, dtype=BF16)


def _qk_norm_rope(t, gain, gmat, cos, sina, sinb):
    sq = t * t
    hi = sq.astype(BF16)
    lo = (sq - hi.astype(F32)).astype(BF16)
    ms = jnp.dot(hi, gmat, preferred_element_type=F32) + jnp.dot(lo, gmat, preferred_element_type=F32)
    tn = t * lax.rsqrt(ms + NORM_EPS) * gain
    half = QK_HEAD_DIM // 2
    return tn * cos + pltpu.roll(tn, V_HEAD_DIM - half, 1) * sina + pltpu.roll(tn, half, 1) * sinb


def _in_proj_body(x_ref, g_ref, w_ref, gq_ref, gk_ref, gmat_ref, cos_ref, sina_ref, sinb_ref,
                  u_ref, qt_ref, k_ref, vt_ref, *, q_scale):
    n = N_HEADS * V_HEAD_DIM
    xn = _rms_rows(x_ref[...], g_ref[...]).astype(BF16)

    def group(c):
        return jnp.dot(xn, w_ref[:, c * n:(c + 1) * n], preferred_element_type=F32)

    u_ref[...] = group(0).astype(BF16)
    gmat, cos, sina, sinb = gmat_ref[...], cos_ref[...], sina_ref[...], sinb_ref[...]
    pq = group(1)
    for h in range(N_HEADS):
        t = _qk_norm_rope(pq[:, h * V_HEAD_DIM:(h + 1) * V_HEAD_DIM], gq_ref[...], gmat, cos, sina, sinb)
        qt_ref[h] = (t * q_scale).T.astype(BF16)
    pk = group(2)
    for h in range(N_HEADS):
        sl = slice(h * V_HEAD_DIM, (h + 1) * V_HEAD_DIM)
        k_ref[:, sl] = _qk_norm_rope(pk[:, sl], gk_ref[...], gmat, cos, sina, sinb).astype(BF16)
    pv = group(3)
    for h in range(N_HEADS):
        vt_ref[h, 0] = pv[:, h * V_HEAD_DIM:(h + 1) * V_HEAD_DIM].T.astype(BF16)


def _in_proj(x, g, w_in, layer, gq, gk, gmat, cos, sina, sinb, q_scale):
    s, d = x.shape
    n = N_HEADS * V_HEAD_DIM
    tm = min(ROW_TILE, s)
    assert tm == min(KV_TILE, s) and w_in.shape[-1] == 4 * n
    row_tab = pl.BlockSpec((tm, V_HEAD_DIM), lambda i: (i, 0))
    head_vec = pl.BlockSpec((1, V_HEAD_DIM), lambda i: (0, 0))
    return pl.pallas_call(
        functools.partial(_in_proj_body, q_scale=q_scale),
        grid=(s // tm,),
        in_specs=[
            pl.BlockSpec((tm, d), lambda i: (i, 0)),
            pl.BlockSpec((1, d), lambda i: (0, 0)),
            pl.BlockSpec((None, d, 4 * n), lambda i: (layer, 0, 0)),
            head_vec, head_vec,
            pl.BlockSpec((V_HEAD_DIM, V_HEAD_DIM), lambda i: (0, 0)),
            row_tab, row_tab, row_tab,
        ],
        out_specs=[
            pl.BlockSpec((tm, n), lambda i: (i, 0)),
            pl.BlockSpec((N_HEADS, V_HEAD_DIM, tm), lambda i: (0, 0, i)),
            pl.BlockSpec((tm, n), lambda i: (i, 0)),
            pl.BlockSpec((N_HEADS, 1, V_HEAD_DIM, tm), lambda i: (0, i, 0, 0)),
        ],
        out_shape=[
            jax.ShapeDtypeStruct((s, n), BF16),
            jax.ShapeDtypeStruct((N_HEADS, V_HEAD_DIM, s), BF16),
            jax.ShapeDtypeStruct((s, n), BF16),
            jax.ShapeDtypeStruct((N_HEADS, s // tm, V_HEAD_DIM, tm), BF16),
        ],
        compiler_params=_params("parallel"),
        name="in_proj",
    )(x, g, w_in, gq, gk, gmat, cos, sina, sinb)


def _dft_tables(s):
    n1, n2, tb = s // DFT_S2, DFT_S2, F1_S2_TILE
    a1 = 2.0 * np.pi * np.outer(np.arange(n1), np.arange(n1)) / n1
    cs1 = np.concatenate([np.cos(a1), -np.sin(a1)], axis=0)
    k1mat = np.kron(cs1, np.eye(tb))
    k = np.arange(n1)[:, None, None] + n1 * np.arange(n2)[None, :, None]
    a2 = 2.0 * np.pi * np.arange(n2)[None, None, :] * k / s
    c2, s2 = np.cos(a2), np.sin(a2)
    g2 = np.concatenate([np.concatenate([c2, s2], axis=2), np.concatenate([-s2, c2], axis=2)], axis=1)
    ac = 2.0 * np.pi * np.outer(np.arange(FOURIER_GROUP_DIM), np.arange(FOURIER_GROUP_DIM)) / FOURIER_GROUP_DIM
    cc = np.concatenate([np.cos(ac), np.sin(ac)], axis=0)
    return jnp.asarray(k1mat, dtype=BF16), jnp.asarray(g2, dtype=BF16), jnp.asarray(cc, dtype=BF16)


def _f1_body(u_ref, k_ref, z_ref):
    n1, tb, ncol = u_ref.shape
    z = jnp.dot(k_ref[...], u_ref[...].reshape(n1 * tb, ncol), preferred_element_type=F32)
    z_ref[...] = z.astype(BF16).reshape(2, n1, tb, ncol)


def _f2_body(z_ref, g_ref, cc_ref, y_ref, *, norm):
    gd = FOURIER_GROUP_DIM
    kb, n2, ncol = z_ref.shape[1], z_ref.shape[2], z_ref.shape[3]
    for kk in range(kb):
        zz = jnp.concatenate([z_ref[0, kk], z_ref[1, kk]], axis=0)
        xb = jnp.dot(g_ref[kk], zz, preferred_element_type=F32).astype(BF16)
        for g in range(ncol // gd):
            lhs = jnp.concatenate([xb[0:n2, g * gd:(g + 1) * gd], xb[n2:2 * n2, g * gd:(g + 1) * gd]], axis=1)
            y = jnp.dot(lhs, cc_ref[...], preferred_element_type=F32)
            y_ref[:, kk * ncol + g * gd:kk * ncol + (g + 1) * gd] = (y * norm).astype(BF16)


def _fourier_mix(u, tables):
    s, ncol = u.shape
    n1, n2 = s // DFT_S2, DFT_S2
    k1mat, g2, cc = tables
    tb = F1_S2_TILE
    z = pl.pallas_call(
        _f1_body,
        grid=(n2 // tb,),
        in_specs=[
            pl.BlockSpec((n1, tb, ncol), lambda t: (0, t, 0)),
            pl.BlockSpec((2 * n1 * tb, n1 * tb), lambda t: (0, 0)),
        ],
        out_specs=pl.BlockSpec((2, n1, tb, ncol), lambda t: (0, 0, t, 0)),
        out_shape=jax.ShapeDtypeStruct((2, n1, n2, ncol), BF16),
        compiler_params=_params("parallel"),
        name="dft_stage1",
    )(u.reshape(n1, n2, ncol), k1mat)
    norm = 1.0 / math.sqrt(s * FOURIER_GROUP_DIM)
    kb = math.gcd(n1, F2_K1_PER_STEP)
    y = pl.pallas_call(
        functools.partial(_f2_body, norm=norm),
        grid=(n1 // kb,),
        in_specs=[
            pl.BlockSpec((2, kb, n2, ncol), lambda k: (0, k, 0, 0)),
            pl.BlockSpec((kb, 2 * n2, 2 * n2), lambda k: (k, 0, 0)),
            pl.BlockSpec((2 * FOURIER_GROUP_DIM, FOURIER_GROUP_DIM), lambda k: (0, 0)),
        ],
        out_specs=pl.BlockSpec((n2, kb * ncol), lambda k: (0, k)),
        out_shape=jax.ShapeDtypeStruct((n2, n1 * ncol), BF16),
        compiler_params=_params("parallel"),
        name="dft_stage2",
    )(z, g2, cc)
    return y.reshape(s, ncol)


def _score_bound(gq, gk):
    bq = jnp.max(jnp.abs(gq), axis=-1, keepdims=True)
    bk = jnp.max(jnp.abs(gk), axis=-1, keepdims=True)
    return (QK_HEAD_DIM * (QK_HEAD_DIM ** -0.5) * LOG2_E) * bq * bk


def _for_each_q_tile(qt_ref, o_ref, tile_fn):
    tq = min(Q_TILE, qt_ref.shape[2])

    def sub(i, carry):
        off = pl.multiple_of(i * tq, tq)
        tile_fn(qt_ref[0, :, pl.ds(off, tq)], o_ref.at[pl.ds(off, tq), :])
        return carry

    lax.fori_loop(0, qt_ref.shape[2] // tq, sub, 0)


def _attn_prologue(qt):
    row = lax.broadcasted_iota(jnp.int32, qt.shape, 0)
    zero = jnp.zeros_like(qt)
    return jnp.where(row < QK_HEAD_DIM, qt, zero), jnp.where(row >= QK_HEAD_DIM, qt, zero)


def _attn_epilogue(o1, o2, lq1_ref, lk1_ref, lq2_ref, lk2_ref, gsub_ref, o_ref, lambda_init):
    lam = (jnp.exp(jnp.sum(lq1_ref[...] * lk1_ref[...], axis=-1, keepdims=True))
           - jnp.exp(jnp.sum(lq2_ref[...] * lk2_ref[...], axis=-1, keepdims=True)) + lambda_init)
    o = o1 - lam * o2
    ms = jnp.mean(o * o, axis=0, keepdims=True)
    y = o * lax.rsqrt(ms + SUBLN_EPS) * gsub_ref[...] * (1.0 - lambda_init)
    o_ref[...] = y.T.astype(BF16)


def _attn_online_body(qt_ref, k_ref, vt_ref, gq_ref, gk_ref, lq1_ref, lk1_ref, lq2_ref, lk2_ref, gsub_ref, o_ref,
                      m_ref, l_ref, acc_ref, *, lambda_init):
    del gq_ref, gk_ref
    nk, tk = vt_ref.shape[1], vt_ref.shape[3]

    def tile(qt, o_view):
        q_halves = _attn_prologue(qt)
        m_ref[...] = jnp.full(m_ref.shape, M_INIT, F32)
        l_ref[...] = jnp.zeros(l_ref.shape, F32)
        acc_ref[...] = jnp.zeros(acc_ref.shape, F32)

        def chunk(c, carry):
            off = pl.multiple_of(c * tk, tk)
            kc = k_ref[pl.ds(off, tk), :]
            vc = vt_ref[0, c]
            for hf in range(2):
                s = jnp.dot(kc, q_halves[hf], preferred_element_type=F32)
                m_old = m_ref[hf]
                m_new = jnp.maximum(m_old, jnp.max(s, axis=0, keepdims=True))
                alpha = jnp.exp2(m_old - m_new)
                p = jnp.exp2(s - m_new)
                l_ref[hf] = alpha * l_ref[hf] + jnp.sum(p, axis=0, keepdims=True)
                acc_ref[hf] = alpha * acc_ref[hf] + jnp.dot(vc, p.astype(BF16), preferred_element_type=F32)
                m_ref[hf] = m_new
            return carry

        lax.fori_loop(0, nk, chunk, 0)
        _attn_epilogue(acc_ref[0] / l_ref[0], acc_ref[1] / l_ref[1],
                       lq1_ref, lk1_ref, lq2_ref, lk2_ref, gsub_ref, o_view, lambda_init)

    _for_each_q_tile(qt_ref, o_ref, tile)


def _attn_bounded_body(qt_ref, k_ref, vt_ref, gq_ref, gk_ref, lq1_ref, lk1_ref, lq2_ref, lk2_ref, gsub_ref, o_ref,
                       l_ref, acc_ref, *, lambda_init):
    nk, tk = vt_ref.shape[1], vt_ref.shape[3]
    bound = _score_bound(gq_ref[...], gk_ref[...])

    per_chunk = tk // KV_UNIT
    unroll = math.gcd(nk, KV_UNROLL)
    units = [(cc, j, hf) for cc in range(unroll) for j in range(per_chunk) for hf in range(2)]

    tq = min(Q_TILE, qt_ref.shape[2])
    n_tiles = qt_ref.shape[2] // tq

    def q_rows(i):
        return pl.ds(i * tq if isinstance(i, int) else pl.multiple_of(i * tq, tq), tq)

    def accumulate(i):
        q_halves = _attn_prologue(qt_ref[0, :, q_rows(i)])

        def trip(t, first_trip):
            def scores(cc, j, hf):
                off = pl.multiple_of((t * unroll + cc) * tk + j * KV_UNIT, KV_UNIT)
                return jnp.dot(k_ref[pl.ds(off, KV_UNIT), :], q_halves[hf], preferred_element_type=F32)

            pending = [scores(*u) for u in units[:QK_LOOKAHEAD]]
            for n, (cc, j, hf) in enumerate(units):
                s = pending.pop(0)
                if n + QK_LOOKAHEAD < len(units):
                    pending.append(scores(*units[n + QK_LOOKAHEAD]))
                p = jnp.exp2(s - bound)
                psum = jnp.sum(p.reshape(KV_UNIT // 8, 8, tq), axis=0)
                vc = vt_ref[0, t * unroll + cc, :, j * KV_UNIT:(j + 1) * KV_UNIT]
                pv = jnp.dot(vc, p.astype(BF16), preferred_element_type=F32)
                if first_trip and n < 2:
                    l_ref[hf] = psum
                    acc_ref[hf] = pv
                else:
                    l_ref[hf] += psum
                    acc_ref[hf] += pv

        trip(0, True)
        if nk // unroll > 1:
            lax.fori_loop(1, nk // unroll, lambda t, c: (trip(t, False), c)[1], 0)

    def finish(i):
        l1 = jnp.sum(l_ref[0], axis=0, keepdims=True)
        l2 = jnp.sum(l_ref[1], axis=0, keepdims=True)
        _attn_epilogue(acc_ref[0] / l1, acc_ref[1] / l2, lq1_ref, lk1_ref, lq2_ref, lk2_ref, gsub_ref,
                       o_ref.at[q_rows(i), :], lambda_init)

    accumulate(0)

    def step(i, carry):
        finish(i - 1)
        accumulate(i)
        return carry

    lax.fori_loop(1, n_tiles, step, 0)
    finish(n_tiles - 1)


def _diff_attention(qt, k, vt, gq, gk, lq1, lk1, lq2, lk2, gsub, lambda_init, online):
    h, _, s = qt.shape
    nk, tk = vt.shape[1], vt.shape[3]
    tq = min(Q_TILE, s)
    tqb = min(tq * Q_TILES_PER_STEP, s)
    vec = pl.BlockSpec((1, QK_HEAD_DIM), lambda hh, i: (0, 0))
    if online:
        body = _attn_online_body
        scratch = [pltpu.VMEM((2, 1, tq), F32), pltpu.VMEM((2, 1, tq), F32), pltpu.VMEM((2, V_HEAD_DIM, tq), F32)]
    else:
        body = _attn_bounded_body
        scratch = [pltpu.VMEM((2, 8, tq), F32), pltpu.VMEM((2, V_HEAD_DIM, tq), F32)]
    return pl.pallas_call(
        functools.partial(body, lambda_init=lambda_init),
        grid=(h, s // tqb),
        in_specs=[
            pl.BlockSpec((1, V_HEAD_DIM, tqb), lambda hh, i: (hh, 0, i)),
            pl.BlockSpec((s, V_HEAD_DIM), lambda hh, i: (0, hh)),
            pl.BlockSpec((1, nk, V_HEAD_DIM, tk), lambda hh, i: (hh, 0, 0, 0)),
            vec, vec, vec, vec, vec, vec,
            pl.BlockSpec((V_HEAD_DIM, 1), lambda hh, i: (0, 0)),
        ],
        out_specs=pl.BlockSpec((tqb, V_HEAD_DIM), lambda hh, i: (i, hh)),
        out_shape=jax.ShapeDtypeStruct((s, h * V_HEAD_DIM), BF16),
        scratch_shapes=scratch,
        compiler_params=_params("parallel", "parallel"),
        name="diff_attention_online" if online else "diff_attention_bounded",
    )(qt, k, vt, gq, gk, lq1, lk1, lq2, lk2, gsub)


def _outproj_body(x_ref, yf_ref, ya_ref, wf_ref, wa_ref, o_ref):
    o_ref[...] = (x_ref[...]
                  + jnp.dot(yf_ref[...], wf_ref[...], preferred_element_type=F32)
                  + jnp.dot(ya_ref[...], wa_ref[...], preferred_element_type=F32))


def _outproj(x, yf, ya, w_out, layer):
    s, d = x.shape
    nf, na = yf.shape[1], ya.shape[1]
    assert nf == na
    tm = min(ROW_TILE, s)
    return pl.pallas_call(
        _outproj_body,
        grid=(s // tm,),
        in_specs=[
            pl.BlockSpec((tm, d), lambda i: (i, 0)),
            pl.BlockSpec((tm, nf), lambda i: (i, 0)),
            pl.BlockSpec((tm, na), lambda i: (i, 0)),
            pl.BlockSpec((None, nf, d), lambda i: (layer, 0, 0)),
            pl.BlockSpec((None, na, d), lambda i: (layer, 1, 0)),
        ],
        out_specs=pl.BlockSpec((tm, d), lambda i: (i, 0)),
        out_shape=jax.ShapeDtypeStruct((s, d), F32),
        compiler_params=_params("parallel"),
        name="out_proj",
    )(x, yf, ya, w_out, w_out)


def kernel(x, positions, norm_ffn1, w1_gate, w1_up, w1_down, norm_mix, w_in, q_norm, k_norm,
           lambda_q1, lambda_k1, lambda_q2, lambda_k2, subln, w_out, norm_ffn2, w2_gate, w2_up, w2_down):
    b, s, d = x.shape
    depth = w_in.shape[0]
    assert b == 1 and s % DFT_S2 == 0
    xs = x.reshape(s, d)
    win, wout = w_in.astype(BF16), w_out.astype(BF16)

    cos, sina, sinb = _rope_tables(positions.reshape(s))
    gmat = _group_mean_matrix()
    tables = _dft_tables(s)
    q_scale = (QK_HEAD_DIM ** -0.5) * LOG2_E

    for l in range(depth):
        lambda_init = 0.8 - 0.6 * math.exp(-0.3 * l)
        xs = _ffn(xs, norm_ffn1[l].reshape(1, d), w1_gate, w1_up, w1_down, l)

        g_mix = norm_mix[l].reshape(1, d)
        gq = jnp.tile(q_norm[l], 2).reshape(1, V_HEAD_DIM)
        gk = jnp.tile(k_norm[l], 2).reshape(1, V_HEAD_DIM)
        u_f, qt, kk, vt = _in_proj(xs, g_mix, win, l, gq, gk, gmat, cos, sina, sinb, q_scale)
        y_f = _fourier_mix(u_f, tables)
        gq64, gk64 = q_norm[l].reshape(1, QK_HEAD_DIM), k_norm[l].reshape(1, QK_HEAD_DIM)
        attn_args = (qt, kk, vt, gq64, gk64,
                     lambda_q1[l].reshape(1, -1), lambda_k1[l].reshape(1, -1),
                     lambda_q2[l].reshape(1, -1), lambda_k2[l].reshape(1, -1),
                     subln[l].reshape(V_HEAD_DIM, 1))
        y_a = lax.cond(
            _score_bound(gq64, gk64)[0, 0] <= MAX_BOUNDED_SOFTMAX_LOG2,
            lambda *a: _diff_attention(*a, lambda_init, False),
            lambda *a: _diff_attention(*a, lambda_init, True),
            *attn_args)
        xs = _outproj(xs, y_f, y_a, wout, l)

        xs = _ffn(xs, norm_ffn2[l].reshape(1, d), w2_gate, w2_up, w2_down, l)
    return xs.reshape(b, s, d)
```

```python
import functools
import math

import numpy as np
import jax
import jax.numpy as jnp
from jax import lax
from jax.experimental import pallas as pl
from jax.experimental.pallas import tpu as pltpu

F32 = jnp.float32
BF16 = jnp.bfloat16

N_HEADS = 8
V_HEAD_DIM = 128
QK_HEAD_DIM = 64
N_FOURIER_GROUPS = 8
FOURIER_GROUP_DIM = 128
ROPE_THETA = 10000.0
NORM_EPS = 1e-6
SUBLN_EPS = 1e-5
LOG2_E = 1.4426950408889634

V7X_VMEM_BYTES = 64 * 1024 * 1024
VMEM_LIMIT_BYTES = V7X_VMEM_BYTES - 4 * 1024 * 1024

ROW_TILE = 512
FFN_ROW_TILE = 1024
FF_TILE = 512
Q_TILE = 512
Q_TILES_PER_STEP = 8
KV_TILE = 512
DFT_S2 = 128
F1_S2_TILE = 16
F2_K1_PER_STEP = 8
KV_UNROLL = 16
KV_UNIT = 256
QK_LOOKAHEAD = 2
M_INIT = -1e30
MAX_BOUNDED_SOFTMAX_LOG2 = 60.0


def _params(*semantics):
    return pltpu.CompilerParams(dimension_semantics=semantics, vmem_limit_bytes=VMEM_LIMIT_BYTES)


def _rms_rows(x, g):
    ms = jnp.mean(x * x, axis=-1, keepdims=True)
    return x * lax.rsqrt(ms + NORM_EPS) * g


def _ffn_body(x_hbm, g_ref, wg_ref, wu_ref, wd_ref, o_ref, xrow_ref, xn_ref, x_sem):
    i, j = pl.program_id(0), pl.program_id(1)
    tm = o_ref.shape[0]

    def x_tile_copy(tile):
        return pltpu.make_async_copy(x_hbm.at[pl.ds(tile * tm, tm), :], xrow_ref, x_sem)

    @pl.when(jnp.logical_and(i == 0, j == 0))
    def _():
        x_tile_copy(0).start()

    @pl.when(j == 0)
    def _():
        x_tile_copy(i).wait()
        x = xrow_ref[...]
        xn_ref[...] = _rms_rows(x, g_ref[...]).astype(BF16)
        o_ref[...] = x

        @pl.when(i + 1 < pl.num_programs(0))
        def _():
            x_tile_copy(i + 1).start()

    xn = xn_ref[...]
    h = jnp.dot(xn, wg_ref[...].astype(BF16), preferred_element_type=F32)
    u = jnp.dot(xn, wu_ref[...].astype(BF16), preferred_element_type=F32)
    a = (h * jax.nn.sigmoid(h) * u).astype(BF16)
    o_ref[...] += 0.5 * jnp.dot(a, wd_ref[...].astype(BF16), preferred_element_type=F32)


def _ffn(x, g, wg, wu, wd, layer):
    s, d = x.shape
    f = wg.shape[-1]
    tm, tf = min(FFN_ROW_TILE, s), min(FF_TILE, f)
    return pl.pallas_call(
        _ffn_body,
        grid=(s // tm, f // tf),
        in_specs=[
            pl.BlockSpec(memory_space=pl.ANY),
            pl.BlockSpec((1, d), lambda i, j: (0, 0)),
            pl.BlockSpec((None, d, tf), lambda i, j: (layer, 0, j)),
            pl.BlockSpec((None, d, tf), lambda i, j: (layer, 0, j)),
            pl.BlockSpec((None, tf, d), lambda i, j: (layer, j, 0)),
        ],
        out_specs=pl.BlockSpec((tm, d), lambda i, j: (i, 0)),
        out_shape=jax.ShapeDtypeStruct((s, d), F32),
        scratch_shapes=[pltpu.VMEM((tm, d), F32), pltpu.VMEM((tm, d), BF16), pltpu.SemaphoreType.DMA(())],
        compiler_params=_params("arbitrary", "arbitrary"),
        name="ffn",
    )(x, g, wg, wu, wd)


def _rope_body(pos_ref, invf_ref, cos_ref, sina_ref, sinb_ref):
    ang = pos_ref[...] * invf_ref[...]
    c = jnp.cos(ang)
    s = jnp.sin(ang)
    lane = lax.broadcasted_iota(jnp.int32, ang.shape, 1)
    first = (lane & (QK_HEAD_DIM // 2)) == 0
    cos_ref[...] = c
    sina_ref[...] = jnp.where(first, -s, 0.0)
    sinb_ref[...] = jnp.where(first, 0.0, s)


def _rope_tables(positions):
    s = positions.shape[0]
    inv_freq = 1.0 / (ROPE_THETA ** (jnp.arange(0, QK_HEAD_DIM, 2, dtype=F32) / QK_HEAD_DIM))
    invf = jnp.tile(inv_freq, 2 * V_HEAD_DIM // QK_HEAD_DIM).reshape(1, V_HEAD_DIM)
    pos = positions.astype(F32).reshape(s, 1)
    ts = min(1024, s)
    tab = jax.ShapeDtypeStruct((s, V_HEAD_DIM), F32)
    return pl.pallas_call(
        _rope_body,
        grid=(s // ts,),
        in_specs=[pl.BlockSpec((ts, 1), lambda i: (i, 0)), pl.BlockSpec((1, V_HEAD_DIM), lambda i: (0, 0))],
        out_specs=[pl.BlockSpec((ts, V_HEAD_DIM), lambda i: (i, 0))] * 3,
        out_shape=[tab, tab, tab],
        compiler_params=_params("parallel"),
        name="rope_tables",
    )(pos, invf)


def _group_mean_matrix():
    idx = np.arange(V_HEAD_DIM) // QK_HEAD_DIM
    return jnp.asarray((idx[:, None] == idx[None, :]).astype(np.float32) / QK_HEAD_DIM, dtype=BF16)


def _qk_norm_rope(t, gain, gmat, cos, sina, sinb):
    sq = t * t
    hi = sq.astype(BF16)
    lo = (sq - hi.astype(F32)).astype(BF16)
    ms = jnp.dot(hi, gmat, preferred_element_type=F32) + jnp.dot(lo, gmat, preferred_element_type=F32)
    tn = t * lax.rsqrt(ms + NORM_EPS) * gain
    half = QK_HEAD_DIM // 2
    return tn * cos + pltpu.roll(tn, V_HEAD_DIM - half, 1) * sina + pltpu.roll(tn, half, 1) * sinb


def _in_proj_body(x_ref, g_ref, w_ref, gq_ref, gk_ref, gmat_ref, cos_ref, sina_ref, sinb_ref,
                  u_ref, qt_ref, k_ref, vt_ref, *, q_scale):
    n = N_HEADS * V_HEAD_DIM
    xn = _rms_rows(x_ref[...], g_ref[...]).astype(BF16)

    def group(c):
        return jnp.dot(xn, w_ref[:, c * n:(c + 1) * n], preferred_element_type=F32)

    u_ref[...] = group(0).astype(BF16)
    gmat, cos, sina, sinb = gmat_ref[...], cos_ref[...], sina_ref[...], sinb_ref[...]
    pq = group(1)
    for h in range(N_HEADS):
        t = _qk_norm_rope(pq[:, h * V_HEAD_DIM:(h + 1) * V_HEAD_DIM], gq_ref[...], gmat, cos, sina, sinb)
        qt_ref[h] = (t * q_scale).T.astype(BF16)
    pk = group(2)
    for h in range(N_HEADS):
        sl = slice(h * V_HEAD_DIM, (h + 1) * V_HEAD_DIM)
        k_ref[:, sl] = _qk_norm_rope(pk[:, sl], gk_ref[...], gmat, cos, sina, sinb).astype(BF16)
    pv = group(3)
    for h in range(N_HEADS):
        vt_ref[h, 0] = pv[:, h * V_HEAD_DIM:(h + 1) * V_HEAD_DIM].T.astype(BF16)


def _in_proj(x, g, w_in, layer, gq, gk, gmat, cos, sina, sinb, q_scale):
    s, d = x.shape
    n = N_HEADS * V_HEAD_DIM
    tm = min(ROW_TILE, s)
    assert tm == min(KV_TILE, s) and w_in.shape[-1] == 4 * n
    row_tab = pl.BlockSpec((tm, V_HEAD_DIM), lambda i: (i, 0))
    head_vec = pl.BlockSpec((1, V_HEAD_DIM), lambda i: (0, 0))
    return pl.pallas_call(
        functools.partial(_in_proj_body, q_scale=q_scale),
        grid=(s // tm,),
        in_specs=[
            pl.BlockSpec((tm, d), lambda i: (i, 0)),
            pl.BlockSpec((1, d), lambda i: (0, 0)),
            pl.BlockSpec((None, d, 4 * n), lambda i: (layer, 0, 0)),
            head_vec, head_vec,
            pl.BlockSpec((V_HEAD_DIM, V_HEAD_DIM), lambda i: (0, 0)),
            row_tab, row_tab, row_tab,
        ],
        out_specs=[
            pl.BlockSpec((tm, n), lambda i: (i, 0)),
            pl.BlockSpec((N_HEADS, V_HEAD_DIM, tm), lambda i: (0, 0, i)),
            pl.BlockSpec((tm, n), lambda i: (i, 0)),
            pl.BlockSpec((N_HEADS, 1, V_HEAD_DIM, tm), lambda i: (0, i, 0, 0)),
        ],
        out_shape=[
            jax.ShapeDtypeStruct((s, n), BF16),
            jax.ShapeDtypeStruct((N_HEADS, V_HEAD_DIM, s), BF16),
            jax.ShapeDtypeStruct((s, n), BF16),
            jax.ShapeDtypeStruct((N_HEADS, s // tm, V_HEAD_DIM, tm), BF16),
        ],
        compiler_params=_params("parallel"),
        name="in_proj",
    )(x, g, w_in, gq, gk, gmat, cos, sina, sinb)


def _dft_tables(s):
    n1, n2, tb = s // DFT_S2, DFT_S2, F1_S2_TILE
    a1 = 2.0 * np.pi * np.outer(np.arange(n1), np.arange(n1)) / n1
    cs1 = np.concatenate([np.cos(a1), -np.sin(a1)], axis=0)
    k1mat = np.kron(cs1, np.eye(tb))
    k = np.arange(n1)[:, None, None] + n1 * np.arange(n2)[None, :, None]
    a2 = 2.0 * np.pi * np.arange(n2)[None, None, :] * k / s
    c2, s2 = np.cos(a2), np.sin(a2)
    g2 = np.concatenate([np.concatenate([c2, s2], axis=2), np.concatenate([-s2, c2], axis=2)], axis=1)
    ac = 2.0 * np.pi * np.outer(np.arange(FOURIER_GROUP_DIM), np.arange(FOURIER_GROUP_DIM)) / FOURIER_GROUP_DIM
    cc = np.concatenate([np.cos(ac), np.sin(ac)], axis=0)
    return tuple(jnp.asarray(t, dtype=F32).astype(BF16) for t in (k1mat, g2, cc))


def _f1_body(u_ref, k_ref, z_ref):
    n1, tb, ncol = u_ref.shape
    z = jnp.dot(k_ref[...], u_ref[...].reshape(n1 * tb, ncol), preferred_element_type=F32)
    z_ref[...] = z.astype(BF16).reshape(2, n1, tb, ncol)


def _f2_body(z_ref, g_ref, cc_ref, y_ref, *, norm):
    gd = FOURIER_GROUP_DIM
    kb, n2, ncol = z_ref.shape[1], z_ref.shape[2], z_ref.shape[3]
    for kk in range(kb):
        zz = jnp.concatenate([z_ref[0, kk], z_ref[1, kk]], axis=0)
        xb = jnp.dot(g_ref[kk], zz, preferred_element_type=F32).astype(BF16)
        for g in range(ncol // gd):
            lhs = jnp.concatenate([xb[0:n2, g * gd:(g + 1) * gd], xb[n2:2 * n2, g * gd:(g + 1) * gd]], axis=1)
            y = jnp.dot(lhs, cc_ref[...], preferred_element_type=F32)
            y_ref[:, kk * ncol + g * gd:kk * ncol + (g + 1) * gd] = (y * norm).astype(BF16)


def _fourier_mix(u, tables):
    s, ncol = u.shape
    n1, n2 = s // DFT_S2, DFT_S2
    k1mat, g2, cc = tables
    tb = F1_S2_TILE
    z = pl.pallas_call(
        _f1_body,
        grid=(n2 // tb,),
        in_specs=[
            pl.BlockSpec((n1, tb, ncol), lambda t: (0, t, 0)),
            pl.BlockSpec((2 * n1 * tb, n1 * tb), lambda t: (0, 0)),
        ],
        out_specs=pl.BlockSpec((2, n1, tb, ncol), lambda t: (0, 0, t, 0)),
        out_shape=jax.ShapeDtypeStruct((2, n1, n2, ncol), BF16),
        compiler_params=_params("parallel"),
        name="dft_stage1",
    )(u.reshape(n1, n2, ncol), k1mat)
    norm = 1.0 / math.sqrt(s * FOURIER_GROUP_DIM)
    kb = math.gcd(n1, F2_K1_PER_STEP)
    y = pl.pallas_call(
        functools.partial(_f2_body, norm=norm),
        grid=(n1 // kb,),
        in_specs=[
            pl.BlockSpec((2, kb, n2, ncol), lambda k: (0, k, 0, 0)),
            pl.BlockSpec((kb, 2 * n2, 2 * n2), lambda k: (k, 0, 0)),
            pl.BlockSpec((2 * FOURIER_GROUP_DIM, FOURIER_GROUP_DIM), lambda k: (0, 0)),
        ],
        out_specs=pl.BlockSpec((n2, kb * ncol), lambda k: (0, k)),
        out_shape=jax.ShapeDtypeStruct((n2, n1 * ncol), BF16),
        compiler_params=_params("parallel"),
        name="dft_stage2",
    )(z, g2, cc)
    return y.reshape(s, ncol)


def _score_bound(gq, gk):
    bq = jnp.max(jnp.abs(gq), axis=-1, keepdims=True)
    bk = jnp.max(jnp.abs(gk), axis=-1, keepdims=True)
    return (QK_HEAD_DIM * (QK_HEAD_DIM ** -0.5) * LOG2_E) * bq * bk


def _for_each_q_tile(qt_ref, o_ref, tile_fn):
    tq = min(Q_TILE, qt_ref.shape[2])

    def sub(i, carry):
        off = pl.multiple_of(i * tq, tq)
        tile_fn(qt_ref[0, :, pl.ds(off, tq)], o_ref.at[pl.ds(off, tq), :])
        return carry

    lax.fori_loop(0, qt_ref.shape[2] // tq, sub, 0)


def _attn_prologue(qt):
    row = lax.broadcasted_iota(jnp.int32, qt.shape, 0)
    zero = jnp.zeros_like(qt)
    return jnp.where(row < QK_HEAD_DIM, qt, zero), jnp.where(row >= QK_HEAD_DIM, qt, zero)


def _attn_epilogue(o1, o2, lq1_ref, lk1_ref, lq2_ref, lk2_ref, gsub_ref, o_ref, lambda_init):
    lam = (jnp.exp(jnp.sum(lq1_ref[...] * lk1_ref[...], axis=-1, keepdims=True))
           - jnp.exp(jnp.sum(lq2_ref[...] * lk2_ref[...], axis=-1, keepdims=True)) + lambda_init)
    o = o1 - lam * o2
    ms = jnp.mean(o * o, axis=0, keepdims=True)
    y = o * lax.rsqrt(ms + SUBLN_EPS) * gsub_ref[...] * (1.0 - lambda_init)
    o_ref[...] = y.T.astype(BF16)


def _attn_online_body(qt_ref, k_ref, vt_ref, gq_ref, gk_ref, lq1_ref, lk1_ref, lq2_ref, lk2_ref, gsub_ref, o_ref,
                      m_ref, l_ref, acc_ref, *, lambda_init):
    del gq_ref, gk_ref
    nk, tk = vt_ref.shape[1], vt_ref.shape[3]

    def tile(qt, o_view):
        q_halves = _attn_prologue(qt)
        m_ref[...] = jnp.full(m_ref.shape, M_INIT, F32)
        l_ref[...] = jnp.zeros(l_ref.shape, F32)
        acc_ref[...] = jnp.zeros(acc_ref.shape, F32)

        def chunk(c, carry):
            off = pl.multiple_of(c * tk, tk)
            kc = k_ref[pl.ds(off, tk), :]
            vc = vt_ref[0, c]
            for hf in range(2):
                s = jnp.dot(kc, q_halves[hf], preferred_element_type=F32)
                m_old = m_ref[hf]
                m_new = jnp.maximum(m_old, jnp.max(s, axis=0, keepdims=True))
                alpha = jnp.exp2(m_old - m_new)
                p = jnp.exp2(s - m_new)
                l_ref[hf] = alpha * l_ref[hf] + jnp.sum(p, axis=0, keepdims=True)
                acc_ref[hf] = alpha * acc_ref[hf] + jnp.dot(vc, p.astype(BF16), preferred_element_type=F32)
                m_ref[hf] = m_new
            return carry

        lax.fori_loop(0, nk, chunk, 0)
        _attn_epilogue(acc_ref[0] / l_ref[0], acc_ref[1] / l_ref[1],
                       lq1_ref, lk1_ref, lq2_ref, lk2_ref, gsub_ref, o_view, lambda_init)

    _for_each_q_tile(qt_ref, o_ref, tile)


def _attn_bounded_body(qt_ref, k_ref, vt_ref, gq_ref, gk_ref, lq1_ref, lk1_ref, lq2_ref, lk2_ref, gsub_ref, o_ref,
                       l_ref, acc_ref, *, lambda_init):
    nk, tk = vt_ref.shape[1], vt_ref.shape[3]
    bound = _score_bound(gq_ref[...], gk_ref[...])

    per_chunk = tk // KV_UNIT
    unroll = math.gcd(nk, KV_UNROLL)
    units = [(cc, j, hf) for cc in range(unroll) for j in range(per_chunk) for hf in range(2)]

    tq = min(Q_TILE, qt_ref.shape[2])
    n_tiles = qt_ref.shape[2] // tq

    def q_rows(i):
        return pl.ds(i * tq if isinstance(i, int) else pl.multiple_of(i * tq, tq), tq)

    def accumulate(i):
        q_halves = _attn_prologue(qt_ref[0, :, q_rows(i)])

        def trip(t, first_trip):
            def scores(cc, j, hf):
                off = pl.multiple_of((t * unroll + cc) * tk + j * KV_UNIT, KV_UNIT)
                return jnp.dot(k_ref[pl.ds(off, KV_UNIT), :], q_halves[hf], preferred_element_type=F32)

            pending = [scores(*u) for u in units[:QK_LOOKAHEAD]]
            for n, (cc, j, hf) in enumerate(units):
                s = pending.pop(0)
                if n + QK_LOOKAHEAD < len(units):
                    pending.append(scores(*units[n + QK_LOOKAHEAD]))
                p = jnp.exp2(s - bound)
                psum = jnp.sum(p.reshape(KV_UNIT // 8, 8, tq), axis=0)
                vc = vt_ref[0, t * unroll + cc, :, j * KV_UNIT:(j + 1) * KV_UNIT]
                pv = jnp.dot(vc, p.astype(BF16), preferred_element_type=F32)
                if first_trip and n < 2:
                    l_ref[hf] = psum
                    acc_ref[hf] = pv
                else:
                    l_ref[hf] += psum
                    acc_ref[hf] += pv

        trip(0, True)
        if nk // unroll > 1:
            lax.fori_loop(1, nk // unroll, lambda t, c: (trip(t, False), c)[1], 0)

    def finish(i):
        l1 = jnp.sum(l_ref[0], axis=0, keepdims=True)
        l2 = jnp.sum(l_ref[1], axis=0, keepdims=True)
        _attn_epilogue(acc_ref[0] / l1, acc_ref[1] / l2, lq1_ref, lk1_ref, lq2_ref, lk2_ref, gsub_ref,
                       o_ref.at[q_rows(i), :], lambda_init)

    accumulate(0)

    def step(i, carry):
        finish(i - 1)
        accumulate(i)
        return carry

    lax.fori_loop(1, n_tiles, step, 0)
    finish(n_tiles - 1)


def _diff_attention(qt, k, vt, gq, gk, lq1, lk1, lq2, lk2, gsub, lambda_init, online):
    h, _, s = qt.shape
    nk, tk = vt.shape[1], vt.shape[3]
    tq = min(Q_TILE, s)
    tqb = min(tq * Q_TILES_PER_STEP, s)
    vec = pl.BlockSpec((1, QK_HEAD_DIM), lambda hh, i: (0, 0))
    if online:
        body = _attn_online_body
        scratch = [pltpu.VMEM((2, 1, tq), F32), pltpu.VMEM((2, 1, tq), F32), pltpu.VMEM((2, V_HEAD_DIM, tq), F32)]
    else:
        body = _attn_bounded_body
        scratch = [pltpu.VMEM((2, 8, tq), F32), pltpu.VMEM((2, V_HEAD_DIM, tq), F32)]
    return pl.pallas_call(
        functools.partial(body, lambda_init=lambda_init),
        grid=(h, s // tqb),
        in_specs=[
            pl.BlockSpec((1, V_HEAD_DIM, tqb), lambda hh, i: (hh, 0, i)),
            pl.BlockSpec((s, V_HEAD_DIM), lambda hh, i: (0, hh)),
            pl.BlockSpec((1, nk, V_HEAD_DIM, tk), lambda hh, i: (hh, 0, 0, 0)),
            vec, vec, vec, vec, vec, vec,
            pl.BlockSpec((V_HEAD_DIM, 1), lambda hh, i: (0, 0)),
        ],
        out_specs=pl.BlockSpec((tqb, V_HEAD_DIM), lambda hh, i: (i, hh)),
        out_shape=jax.ShapeDtypeStruct((s, h * V_HEAD_DIM), BF16),
        scratch_shapes=scratch,
        compiler_params=_params("parallel", "parallel"),
        name="diff_attention_online" if online else "diff_attention_bounded",
    )(qt, k, vt, gq, gk, lq1, lk1, lq2, lk2, gsub)


def _outproj_body(x_ref, yf_ref, ya_ref, wf_ref, wa_ref, o_ref):
    o_ref[...] = (x_ref[...]
                  + jnp.dot(yf_ref[...], wf_ref[...], preferred_element_type=F32)
                  + jnp.dot(ya_ref[...], wa_ref[...], preferred_element_type=F32))


def _outproj(x, yf, ya, w_out, layer):
    s, d = x.shape
    nf, na = yf.shape[1], ya.shape[1]
    assert nf == na
    tm = min(ROW_TILE, s)
    return pl.pallas_call(
        _outproj_body,
        grid=(s // tm,),
        in_specs=[
            pl.BlockSpec((tm, d), lambda i: (i, 0)),
            pl.BlockSpec((tm, nf), lambda i: (i, 0)),
            pl.BlockSpec((tm, na), lambda i: (i, 0)),
            pl.BlockSpec((None, nf, d), lambda i: (layer, 0, 0)),
            pl.BlockSpec((None, na, d), lambda i: (layer, 1, 0)),
        ],
        out_specs=pl.BlockSpec((tm, d), lambda i: (i, 0)),
        out_shape=jax.ShapeDtypeStruct((s, d), F32),
        compiler_params=_params("parallel"),
        name="out_proj",
    )(x, yf, ya, w_out, w_out)


def kernel(x, positions, norm_ffn1, w1_gate, w1_up, w1_down, norm_mix, w_in, q_norm, k_norm,
           lambda_q1, lambda_k1, lambda_q2, lambda_k2, subln, w_out, norm_ffn2, w2_gate, w2_up, w2_down):
    b, s, d = x.shape
    depth = w_in.shape[0]
    assert b == 1 and s % DFT_S2 == 0
    xs = x.reshape(s, d)
    win, wout = w_in.astype(BF16), w_out.astype(BF16)

    cos, sina, sinb = _rope_tables(positions.reshape(s))
    gmat = _group_mean_matrix()
    tables = _dft_tables(s)
    q_scale = (QK_HEAD_DIM ** -0.5) * LOG2_E

    for l in range(depth):
        lambda_init = 0.8 - 0.6 * math.exp(-0.3 * l)
        xs = _ffn(xs, norm_ffn1[l].reshape(1, d), w1_gate, w1_up, w1_down, l)

        g_mix = norm_mix[l].reshape(1, d)
        gq = jnp.tile(q_norm[l], 2).reshape(1, V_HEAD_DIM)
        gk = jnp.tile(k_norm[l], 2).reshape(1, V_HEAD_DIM)
        u_f, qt, kk, vt = _in_proj(xs, g_mix, win, l, gq, gk, gmat, cos, sina, sinb, q_scale)
        y_f = _fourier_mix(u_f, tables)
        gq64, gk64 = q_norm[l].reshape(1, QK_HEAD_DIM), k_norm[l].reshape(1, QK_HEAD_DIM)
        attn_args = (qt, kk, vt, gq64, gk64,
                     lambda_q1[l].reshape(1, -1), lambda_k1[l].reshape(1, -1),
                     lambda_q2[l].reshape(1, -1), lambda_k2[l].reshape(1, -1),
                     subln[l].reshape(V_HEAD_DIM, 1))
        y_a = lax.cond(
            _score_bound(gq64, gk64)[0, 0] <= MAX_BOUNDED_SOFTMAX_LOG2,
            lambda *a: _diff_attention(*a, lambda_init, False),
            lambda *a: _diff_attention(*a, lambda_init, True),
            *attn_args)
        xs = _outproj(xs, y_f, y_a, wout, l)

        xs = _ffn(xs, norm_ffn2[l].reshape(1, d), w2_gate, w2_up, w2_down, l)
    return xs.reshape(b, s, d)
```

```python
import functools
import math

import numpy as np
import jax
import jax.numpy as jnp
from jax import lax
from jax.experimental import pallas as pl
from jax.experimental.pallas import tpu as pltpu

F32 = jnp.float32
BF16 = jnp.bfloat16

N_HEADS = 8
V_HEAD_DIM = 128
QK_HEAD_DIM = 64
N_FOURIER_GROUPS = 8
FOURIER_GROUP_DIM = 128
ROPE_THETA = 10000.0
NORM_EPS = 1e-6
SUBLN_EPS = 1e-5
LOG2_E = 1.4426950408889634

V7X_VMEM_BYTES = 64 * 1024 * 1024
VMEM_LIMIT_BYTES = V7X_VMEM_BYTES - 4 * 1024 * 1024

ROW_TILE = 512
FFN_ROW_TILE = 1024
FF_TILE = 512
Q_TILE = 512
Q_TILES_PER_STEP = 8
KV_TILE = 512
DFT_S2 = 128
F1_S2_TILE = 16
F2_K1_PER_STEP = 8
KV_UNROLL = 16
KV_UNIT = 256
QK_LOOKAHEAD = 2
M_INIT = -1e30
MAX_BOUNDED_SOFTMAX_LOG2 = 60.0


def _params(*semantics):
    return pltpu.CompilerParams(dimension_semantics=semantics, vmem_limit_bytes=VMEM_LIMIT_BYTES)


def _rms_rows(x, g):
    ms = jnp.mean(x * x, axis=-1, keepdims=True)
    return x * lax.rsqrt(ms + NORM_EPS) * g


def _ffn_body(x_hbm, g_ref, wg_ref, wu_ref, wd_ref, o_ref, xrow_ref, xn_ref, x_sem):
    i, j = pl.program_id(0), pl.program_id(1)
    tm = o_ref.shape[0]

    def x_tile_copy(tile):
        return pltpu.make_async_copy(x_hbm.at[pl.ds(tile * tm, tm), :], xrow_ref, x_sem)

    @pl.when(jnp.logical_and(i == 0, j == 0))
    def _():
        x_tile_copy(0).start()

    @pl.when(j == 0)
    def _():
        x_tile_copy(i).wait()
        x = xrow_ref[...]
        xn_ref[...] = _rms_rows(x, g_ref[...]).astype(BF16)
        o_ref[...] = x

        @pl.when(i + 1 < pl.num_programs(0))
        def _():
            x_tile_copy(i + 1).start()

    xn = xn_ref[...]
    h = jnp.dot(xn, wg_ref[...].astype(BF16), preferred_element_type=F32)
    u = jnp.dot(xn, wu_ref[...].astype(BF16), preferred_element_type=F32)
    a = (h * jax.nn.sigmoid(h) * u).astype(BF16)
    o_ref[...] += 0.5 * jnp.dot(a, wd_ref[...].astype(BF16), preferred_element_type=F32)


def _ffn(x, g, wg, wu, wd, layer):
    s, d = x.shape
    f = wg.shape[-1]
    tm, tf = min(FFN_ROW_TILE, s), min(FF_TILE, f)
    return pl.pallas_call(
        _ffn_body,
        grid=(s // tm, f // tf),
        in_specs=[
            pl.BlockSpec(memory_space=pl.ANY),
            pl.BlockSpec((1, d), lambda i, j: (0, 0)),
            pl.BlockSpec((None, d, tf), lambda i, j: (layer, 0, j)),
            pl.BlockSpec((None, d, tf), lambda i, j: (layer, 0, j)),
            pl.BlockSpec((None, tf, d), lambda i, j: (layer, j, 0)),
        ],
        out_specs=pl.BlockSpec((tm, d), lambda i, j: (i, 0)),
        out_shape=jax.ShapeDtypeStruct((s, d), F32),
        scratch_shapes=[pltpu.VMEM((tm, d), F32), pltpu.VMEM((tm, d), BF16), pltpu.SemaphoreType.DMA(())],
        compiler_params=_params("arbitrary", "arbitrary"),
        name="ffn",
    )(x, g, wg, wu, wd)


def _rope_body(pos_ref, invf_ref, cos_ref, sina_ref, sinb_ref):
    ang = pos_ref[...] * invf_ref[...]
    c = jnp.cos(ang)
    s = jnp.sin(ang)
    lane = lax.broadcasted_iota(jnp.int32, ang.shape, 1)
    first = (lane & (QK_HEAD_DIM // 2)) == 0
    cos_ref[...] = c
    sina_ref[...] = jnp.where(first, -s, 0.0)
    sinb_ref[...] = jnp.where(first, 0.0, s)


def _rope_tables(positions):
    s = positions.shape[0]
    inv_freq = 1.0 / (ROPE_THETA ** (jnp.arange(0, QK_HEAD_DIM, 2, dtype=F32) / QK_HEAD_DIM))
    invf = jnp.tile(inv_freq, 2 * V_HEAD_DIM // QK_HEAD_DIM).reshape(1, V_HEAD_DIM)
    pos = positions.astype(F32).reshape(s, 1)
    ts = min(1024, s)
    tab = jax.ShapeDtypeStruct((s, V_HEAD_DIM), F32)
    return pl.pallas_call(
        _rope_body,
        grid=(s // ts,),
        in_specs=[pl.BlockSpec((ts, 1), lambda i: (i, 0)), pl.BlockSpec((1, V_HEAD_DIM), lambda i: (0, 0))],
        out_specs=[pl.BlockSpec((ts, V_HEAD_DIM), lambda i: (i, 0))] * 3,
        out_shape=[tab, tab, tab],
        compiler_params=_params("parallel"),
        name="rope_tables",
    )(pos, invf)


def _group_mean_matrix():
    idx = np.arange(V_HEAD_DIM) // QK_HEAD_DIM
    return jnp.asarray((idx[:, None] == idx[None, :]).astype(np.float32) / QK_HEAD_DIM, dtype=BF16)


def _qk_norm_rope(t, gain, gmat, cos, sina, sinb):
    ms = jnp.dot((t * t).astype(BF16), gmat, preferred_element_type=F32)
    tn = t * lax.rsqrt(ms + NORM_EPS) * gain
    half = QK_HEAD_DIM // 2
    return tn * cos + pltpu.roll(tn, V_HEAD_DIM - half, 1) * sina + pltpu.roll(tn, half, 1) * sinb


def _in_proj_body(x_ref, g_ref, w_ref, gq_ref, gk_ref, gmat_ref, cos_ref, sina_ref, sinb_ref,
                  u_ref, qt_ref, k_ref, vt_ref, *, q_scale):
    n = N_HEADS * V_HEAD_DIM
    xn = _rms_rows(x_ref[...], g_ref[...]).astype(BF16)

    def group(c):
        return jnp.dot(xn, w_ref[:, c * n:(c + 1) * n], preferred_element_type=F32)

    u_ref[...] = group(0).astype(BF16)
    gmat, cos, sina, sinb = gmat_ref[...], cos_ref[...], sina_ref[...], sinb_ref[...]
    pq = group(1)
    for h in range(N_HEADS):
        t = _qk_norm_rope(pq[:, h * V_HEAD_DIM:(h + 1) * V_HEAD_DIM], gq_ref[...], gmat, cos, sina, sinb)
        qt_ref[h] = (t * q_scale).T.astype(BF16)
    pk = group(2)
    for h in range(N_HEADS):
        sl = slice(h * V_HEAD_DIM, (h + 1) * V_HEAD_DIM)
        k_ref[:, sl] = _qk_norm_rope(pk[:, sl], gk_ref[...], gmat, cos, sina, sinb).astype(BF16)
    pv = group(3)
    for h in range(N_HEADS):
        vt_ref[h, 0] = pv[:, h * V_HEAD_DIM:(h + 1) * V_HEAD_DIM].T.astype(BF16)


def _in_proj(x, g, w_in, layer, gq, gk, gmat, cos, sina, sinb, q_scale):
    s, d = x.shape
    n = N_HEADS * V_HEAD_DIM
    tm = min(ROW_TILE, s)
    assert tm == min(KV_TILE, s) and w_in.shape[-1] == 4 * n
    row_tab = pl.BlockSpec((tm, V_HEAD_DIM), lambda i: (i, 0))
    head_vec = pl.BlockSpec((1, V_HEAD_DIM), lambda i: (0, 0))
    return pl.pallas_call(
        functools.partial(_in_proj_body, q_scale=q_scale),
        grid=(s // tm,),
        in_specs=[
            pl.BlockSpec((tm, d), lambda i: (i, 0)),
            pl.BlockSpec((1, d), lambda i: (0, 0)),
            pl.BlockSpec((None, d, 4 * n), lambda i: (layer, 0, 0)),
            head_vec, head_vec,
            pl.BlockSpec((V_HEAD_DIM, V_HEAD_DIM), lambda i: (0, 0)),
            row_tab, row_tab, row_tab,
        ],
        out_specs=[
            pl.BlockSpec((tm, n), lambda i: (i, 0)),
            pl.BlockSpec((N_HEADS, V_HEAD_DIM, tm), lambda i: (0, 0, i)),
            pl.BlockSpec((tm, n), lambda i: (i, 0)),
            pl.BlockSpec((N_HEADS, 1, V_HEAD_DIM, tm), lambda i: (0, i, 0, 0)),
        ],
        out_shape=[
            jax.ShapeDtypeStruct((s, n), BF16),
            jax.ShapeDtypeStruct((N_HEADS, V_HEAD_DIM, s), BF16),
            jax.ShapeDtypeStruct((s, n), BF16),
            jax.ShapeDtypeStruct((N_HEADS, s // tm, V_HEAD_DIM, tm), BF16),
        ],
        compiler_params=_params("parallel"),
        name="in_proj",
    )(x, g, w_in, gq, gk, gmat, cos, sina, sinb)


def _dft_tables(s):
    n1, n2, tb = s // DFT_S2, DFT_S2, F1_S2_TILE
    a1 = 2.0 * np.pi * np.outer(np.arange(n1), np.arange(n1)) / n1
    cs1 = np.concatenate([np.cos(a1), -np.sin(a1)], axis=0)
    k1mat = np.kron(cs1, np.eye(tb))
    k = np.arange(n1)[:, None, None] + n1 * np.arange(n2)[None, :, None]
    a2 = 2.0 * np.pi * np.arange(n2)[None, None, :] * k / s
    c2, s2 = np.cos(a2), np.sin(a2)
    g2 = np.concatenate([np.concatenate([c2, s2], axis=2), np.concatenate([-s2, c2], axis=2)], axis=1)
    ac = 2.0 * np.pi * np.outer(np.arange(FOURIER_GROUP_DIM), np.arange(FOURIER_GROUP_DIM)) / FOURIER_GROUP_DIM
    cc = np.concatenate([np.cos(ac), np.sin(ac)], axis=0)
    return tuple(jnp.asarray(t, dtype=F32).astype(BF16) for t in (k1mat, g2, cc))


def _f1_body(u_ref, k_ref, z_ref):
    n1, tb, ncol = u_ref.shape
    z = jnp.dot(k_ref[...], u_ref[...].reshape(n1 * tb, ncol), preferred_element_type=F32)
    z_ref[...] = z.astype(BF16).reshape(2, n1, tb, ncol)


def _f2_body(z_ref, g_ref, cc_ref, y_ref, *, norm):
    gd = FOURIER_GROUP_DIM
    kb, n2, ncol = z_ref.shape[1], z_ref.shape[2], z_ref.shape[3]
    for kk in range(kb):
        zz = jnp.concatenate([z_ref[0, kk], z_ref[1, kk]], axis=0)
        xb = jnp.dot(g_ref[kk], zz, preferred_element_type=F32).astype(BF16)
        for g in range(ncol // gd):
            lhs = jnp.concatenate([xb[0:n2, g * gd:(g + 1) * gd], xb[n2:2 * n2, g * gd:(g + 1) * gd]], axis=1)
            y = jnp.dot(lhs, cc_ref[...], preferred_element_type=F32)
            y_ref[:, kk * ncol + g * gd:kk * ncol + (g + 1) * gd] = (y * norm).astype(BF16)


def _fourier_mix(u, tables):
    s, ncol = u.shape
    n1, n2 = s // DFT_S2, DFT_S2
    k1mat, g2, cc = tables
    tb = F1_S2_TILE
    z = pl.pallas_call(
        _f1_body,
        grid=(n2 // tb,),
        in_specs=[
            pl.BlockSpec((n1, tb, ncol), lambda t: (0, t, 0)),
            pl.BlockSpec((2 * n1 * tb, n1 * tb), lambda t: (0, 0)),
        ],
        out_specs=pl.BlockSpec((2, n1, tb, ncol), lambda t: (0, 0, t, 0)),
        out_shape=jax.ShapeDtypeStruct((2, n1, n2, ncol), BF16),
        compiler_params=_params("parallel"),
        name="dft_stage1",
    )(u.reshape(n1, n2, ncol), k1mat)
    norm = 1.0 / math.sqrt(s * FOURIER_GROUP_DIM)
    kb = math.gcd(n1, F2_K1_PER_STEP)
    y = pl.pallas_call(
        functools.partial(_f2_body, norm=norm),
        grid=(n1 // kb,),
        in_specs=[
            pl.BlockSpec((2, kb, n2, ncol), lambda k: (0, k, 0, 0)),
            pl.BlockSpec((kb, 2 * n2, 2 * n2), lambda k: (k, 0, 0)),
            pl.BlockSpec((2 * FOURIER_GROUP_DIM, FOURIER_GROUP_DIM), lambda k: (0, 0)),
        ],
        out_specs=pl.BlockSpec((n2, kb * ncol), lambda k: (0, k)),
        out_shape=jax.ShapeDtypeStruct((n2, n1 * ncol), BF16),
        compiler_params=_params("parallel"),
        name="dft_stage2",
    )(z, g2, cc)
    return y.reshape(s, ncol)


def _score_bound(gq, gk):
    bq = jnp.max(jnp.abs(gq), axis=-1, keepdims=True)
    bk = jnp.max(jnp.abs(gk), axis=-1, keepdims=True)
    return (QK_HEAD_DIM * (QK_HEAD_DIM ** -0.5) * LOG2_E) * bq * bk


def _for_each_q_tile(qt_ref, o_ref, tile_fn):
    tq = min(Q_TILE, qt_ref.shape[2])

    def sub(i, carry):
        off = pl.multiple_of(i * tq, tq)
        tile_fn(qt_ref[0, :, pl.ds(off, tq)], o_ref.at[pl.ds(off, tq), :])
        return carry

    lax.fori_loop(0, qt_ref.shape[2] // tq, sub, 0)


def _attn_prologue(qt):
    row = lax.broadcasted_iota(jnp.int32, qt.shape, 0)
    zero = jnp.zeros_like(qt)
    return jnp.where(row < QK_HEAD_DIM, qt, zero), jnp.where(row >= QK_HEAD_DIM, qt, zero)


def _attn_epilogue(o1, o2, lq1_ref, lk1_ref, lq2_ref, lk2_ref, gsub_ref, o_ref, lambda_init):
    lam = (jnp.exp(jnp.sum(lq1_ref[...] * lk1_ref[...], axis=-1, keepdims=True))
           - jnp.exp(jnp.sum(lq2_ref[...] * lk2_ref[...], axis=-1, keepdims=True)) + lambda_init)
    o = o1 - lam * o2
    ms = jnp.mean(o * o, axis=0, keepdims=True)
    y = o * lax.rsqrt(ms + SUBLN_EPS) * gsub_ref[...] * (1.0 - lambda_init)
    o_ref[...] = y.T.astype(BF16)


def _attn_online_body(qt_ref, k_ref, vt_ref, gq_ref, gk_ref, lq1_ref, lk1_ref, lq2_ref, lk2_ref, gsub_ref, o_ref,
                      m_ref, l_ref, acc_ref, *, lambda_init):
    del gq_ref, gk_ref
    nk, tk = vt_ref.shape[1], vt_ref.shape[3]

    def tile(qt, o_view):
        q_halves = _attn_prologue(qt)
        m_ref[...] = jnp.full(m_ref.shape, M_INIT, F32)
        l_ref[...] = jnp.zeros(l_ref.shape, F32)
        acc_ref[...] = jnp.zeros(acc_ref.shape, F32)

        def chunk(c, carry):
            off = pl.multiple_of(c * tk, tk)
            kc = k_ref[pl.ds(off, tk), :]
            vc = vt_ref[0, c]
            for hf in range(2):
                s = jnp.dot(kc, q_halves[hf], preferred_element_type=F32)
                m_old = m_ref[hf]
                m_new = jnp.maximum(m_old, jnp.max(s, axis=0, keepdims=True))
                alpha = jnp.exp2(m_old - m_new)
                p = jnp.exp2(s - m_new)
                l_ref[hf] = alpha * l_ref[hf] + jnp.sum(p, axis=0, keepdims=True)
                acc_ref[hf] = alpha * acc_ref[hf] + jnp.dot(vc, p.astype(BF16), preferred_element_type=F32)
                m_ref[hf] = m_new
            return carry

        lax.fori_loop(0, nk, chunk, 0)
        _attn_epilogue(acc_ref[0] / l_ref[0], acc_ref[1] / l_ref[1],
                       lq1_ref, lk1_ref, lq2_ref, lk2_ref, gsub_ref, o_view, lambda_init)

    _for_each_q_tile(qt_ref, o_ref, tile)


def _attn_bounded_body(qt_ref, k_ref, vt_ref, gq_ref, gk_ref, lq1_ref, lk1_ref, lq2_ref, lk2_ref, gsub_ref, o_ref,
                       l_ref, acc_ref, *, lambda_init):
    nk, tk = vt_ref.shape[1], vt_ref.shape[3]
    bound = _score_bound(gq_ref[...], gk_ref[...])

    per_chunk = tk // KV_UNIT
    unroll = math.gcd(nk, KV_UNROLL)
    units = [(cc, j, hf) for cc in range(unroll) for j in range(per_chunk) for hf in range(2)]

    tq = min(Q_TILE, qt_ref.shape[2])
    n_tiles = qt_ref.shape[2] // tq

    def q_rows(i):
        return pl.ds(i * tq if isinstance(i, int) else pl.multiple_of(i * tq, tq), tq)

    def accumulate(i):
        q_halves = _attn_prologue(qt_ref[0, :, q_rows(i)])

        def trip(t, first_trip):
            def scores(cc, j, hf):
                off = pl.multiple_of((t * unroll + cc) * tk + j * KV_UNIT, KV_UNIT)
                return jnp.dot(k_ref[pl.ds(off, KV_UNIT), :], q_halves[hf], preferred_element_type=F32)

            pending = [scores(*u) for u in units[:QK_LOOKAHEAD]]
            for n, (cc, j, hf) in enumerate(units):
                s = pending.pop(0)
                if n + QK_LOOKAHEAD < len(units):
                    pending.append(scores(*units[n + QK_LOOKAHEAD]))
                p = jnp.exp2(s - bound)
                psum = jnp.sum(p.reshape(KV_UNIT // 8, 8, tq), axis=0)
                vc = vt_ref[0, t * unroll + cc, :, j * KV_UNIT:(j + 1) * KV_UNIT]
                pv = jnp.dot(vc, p.astype(BF16), preferred_element_type=F32)
                if first_trip and n < 2:
                    l_ref[hf] = psum
                    acc_ref[hf] = pv
                else:
                    l_ref[hf] += psum
                    acc_ref[hf] += pv

        trip(0, True)
        if nk // unroll > 1:
            lax.fori_loop(1, nk // unroll, lambda t, c: (trip(t, False), c)[1], 0)

    def finish(i):
        l1 = jnp.sum(l_ref[0], axis=0, keepdims=True)
        l2 = jnp.sum(l_ref[1], axis=0, keepdims=True)
        _attn_epilogue(acc_ref[0] / l1, acc_ref[1] / l2, lq1_ref, lk1_ref, lq2_ref, lk2_ref, gsub_ref,
                       o_ref.at[q_rows(i), :], lambda_init)

    accumulate(0)

    def step(i, carry):
        finish(i - 1)
        accumulate(i)
        return carry

    lax.fori_loop(1, n_tiles, step, 0)
    finish(n_tiles - 1)


def _diff_attention(qt, k, vt, gq, gk, lq1, lk1, lq2, lk2, gsub, lambda_init, online):
    h, _, s = qt.shape
    nk, tk = vt.shape[1], vt.shape[3]
    tq = min(Q_TILE, s)
    tqb = min(tq * Q_TILES_PER_STEP, s)
    vec = pl.BlockSpec((1, QK_HEAD_DIM), lambda hh, i: (0, 0))
    if online:
        body = _attn_online_body
        scratch = [pltpu.VMEM((2, 1, tq), F32), pltpu.VMEM((2, 1, tq), F32), pltpu.VMEM((2, V_HEAD_DIM, tq), F32)]
    else:
        body = _attn_bounded_body
        scratch = [pltpu.VMEM((2, 8, tq), F32), pltpu.VMEM((2, V_HEAD_DIM, tq), F32)]
    return pl.pallas_call(
        functools.partial(body, lambda_init=lambda_init),
        grid=(h, s // tqb),
        in_specs=[
            pl.BlockSpec((1, V_HEAD_DIM, tqb), lambda hh, i: (hh, 0, i)),
            pl.BlockSpec((s, V_HEAD_DIM), lambda hh, i: (0, hh)),
            pl.BlockSpec((1, nk, V_HEAD_DIM, tk), lambda hh, i: (hh, 0, 0, 0)),
            vec, vec, vec, vec, vec, vec,
            pl.BlockSpec((V_HEAD_DIM, 1), lambda hh, i: (0, 0)),
        ],
        out_specs=pl.BlockSpec((tqb, V_HEAD_DIM), lambda hh, i: (i, hh)),
        out_shape=jax.ShapeDtypeStruct((s, h * V_HEAD_DIM), BF16),
        scratch_shapes=scratch,
        compiler_params=_params("parallel", "parallel"),
        name="diff_attention_online" if online else "diff_attention_bounded",
    )(qt, k, vt, gq, gk, lq1, lk1, lq2, lk2, gsub)


def _outproj_body(x_ref, yf_ref, ya_ref, wf_ref, wa_ref, o_ref):
    o_ref[...] = (x_ref[...]
                  + jnp.dot(yf_ref[...], wf_ref[...], preferred_element_type=F32)
                  + jnp.dot(ya_ref[...], wa_ref[...], preferred_element_type=F32))


def _outproj(x, yf, ya, w_out, layer):
    s, d = x.shape
    nf, na = yf.shape[1], ya.shape[1]
    assert nf == na
    tm = min(ROW_TILE, s)
    return pl.pallas_call(
        _outproj_body,
        grid=(s // tm,),
        in_specs=[
            pl.BlockSpec((tm, d), lambda i: (i, 0)),
            pl.BlockSpec((tm, nf), lambda i: (i, 0)),
            pl.BlockSpec((tm, na), lambda i: (i, 0)),
            pl.BlockSpec((None, nf, d), lambda i: (layer, 0, 0)),
            pl.BlockSpec((None, na, d), lambda i: (layer, 1, 0)),
        ],
        out_specs=pl.BlockSpec((tm, d), lambda i: (i, 0)),
        out_shape=jax.ShapeDtypeStruct((s, d), F32),
        compiler_params=_params("parallel"),
        name="out_proj",
    )(x, yf, ya, w_out, w_out)


def kernel(x, positions, norm_ffn1, w1_gate, w1_up, w1_down, norm_mix, w_in, q_norm, k_norm,
           lambda_q1, lambda_k1, lambda_q2, lambda_k2, subln, w_out, norm_ffn2, w2_gate, w2_up, w2_down):
    b, s, d = x.shape
    depth = w_in.shape[0]
    assert b == 1 and s % DFT_S2 == 0
    xs = x.reshape(s, d)
    win, wout = w_in.astype(BF16), w_out.astype(BF16)

    cos, sina, sinb = _rope_tables(positions.reshape(s))
    gmat = _group_mean_matrix()
    tables = _dft_tables(s)
    q_scale = (QK_HEAD_DIM ** -0.5) * LOG2_E

    for l in range(depth):
        lambda_init = 0.8 - 0.6 * math.exp(-0.3 * l)
        xs = _ffn(xs, norm_ffn1[l].reshape(1, d), w1_gate, w1_up, w1_down, l)

        g_mix = norm_mix[l].reshape(1, d)
        gq = jnp.tile(q_norm[l], 2).reshape(1, V_HEAD_DIM)
        gk = jnp.tile(k_norm[l], 2).reshape(1, V_HEAD_DIM)
        u_f, qt, kk, vt = _in_proj(xs, g_mix, win, l, gq, gk, gmat, cos, sina, sinb, q_scale)
        y_f = _fourier_mix(u_f, tables)
        gq64, gk64 = q_norm[l].reshape(1, QK_HEAD_DIM), k_norm[l].reshape(1, QK_HEAD_DIM)
        attn_args = (qt, kk, vt, gq64, gk64,
                     lambda_q1[l].reshape(1, -1), lambda_k1[l].reshape(1, -1),
                     lambda_q2[l].reshape(1, -1), lambda_k2[l].reshape(1, -1),
                     subln[l].reshape(V_HEAD_DIM, 1))
        y_a = lax.cond(
            _score_bound(gq64, gk64)[0, 0] <= MAX_BOUNDED_SOFTMAX_LOG2,
            lambda *a: _diff_attention(*a, lambda_init, False),
            lambda *a: _diff_attention(*a, lambda_init, True),
            *attn_args)
        xs = _outproj(xs, y_f, y_a, wout, l)

        xs = _ffn(xs, norm_ffn2[l].reshape(1, d), w2_gate, w2_up, w2_down, l)
    return xs.reshape(b, s, d)
```

```python
import functools
import math

import numpy as np
import jax
import jax.numpy as jnp
from jax import lax
from jax.experimental import pallas as pl
from jax.experimental.pallas import tpu as pltpu

F32 = jnp.float32
BF16 = jnp.bfloat16

N_HEADS = 8
V_HEAD_DIM = 128
QK_HEAD_DIM = 64
N_FOURIER_GROUPS = 8
FOURIER_GROUP_DIM = 128
ROPE_THETA = 10000.0
NORM_EPS = 1e-6
SUBLN_EPS = 1e-5
LOG2_E = 1.4426950408889634

V7X_VMEM_BYTES = 64 * 1024 * 1024
VMEM_LIMIT_BYTES = V7X_VMEM_BYTES - 4 * 1024 * 1024

ROW_TILE = 512
FFN_ROW_TILE = 1024
FF_TILE = 512
Q_TILE = 512
Q_TILES_PER_STEP = 8
KV_TILE = 512
DFT_S2 = 128
F1_S2_TILE = 16
F2_K1_PER_STEP = 8
KV_UNROLL = 16
KV_UNIT = 256
QK_LOOKAHEAD = 2
M_INIT = -1e30
MAX_BOUNDED_SOFTMAX_LOG2 = 60.0


def _params(*semantics):
    return pltpu.CompilerParams(dimension_semantics=semantics, vmem_limit_bytes=VMEM_LIMIT_BYTES)


def _rms_rows(x, g):
    ms = jnp.mean(x * x, axis=-1, keepdims=True)
    return x * lax.rsqrt(ms + NORM_EPS) * g


def _ffn_body(x_hbm, g_ref, wg_ref, wu_ref, wd_ref, o_ref, xrow_ref, xn_ref, x_sem):
    i, j = pl.program_id(0), pl.program_id(1)
    tm = o_ref.shape[0]

    def x_tile_copy(tile):
        return pltpu.make_async_copy(x_hbm.at[pl.ds(tile * tm, tm), :], xrow_ref, x_sem)

    @pl.when(jnp.logical_and(i == 0, j == 0))
    def _():
        x_tile_copy(0).start()

    @pl.when(j == 0)
    def _():
        x_tile_copy(i).wait()
        x = xrow_ref[...]
        xn_ref[...] = _rms_rows(x, g_ref[...]).astype(BF16)
        o_ref[...] = x

        @pl.when(i + 1 < pl.num_programs(0))
        def _():
            x_tile_copy(i + 1).start()

    xn = xn_ref[...]
    h = jnp.dot(xn, wg_ref[...].astype(BF16), preferred_element_type=F32)
    u = jnp.dot(xn, wu_ref[...].astype(BF16), preferred_element_type=F32)
    a = (h * jax.nn.sigmoid(h) * u).astype(BF16)
    o_ref[...] += 0.5 * jnp.dot(a, wd_ref[...].astype(BF16), preferred_element_type=F32)


def _ffn(x, g, wg, wu, wd, layer):
    s, d = x.shape
    f = wg.shape[-1]
    tm, tf = min(FFN_ROW_TILE, s), min(FF_TILE, f)
    return pl.pallas_call(
        _ffn_body,
        grid=(s // tm, f // tf),
        in_specs=[
            pl.BlockSpec(memory_space=pl.ANY),
            pl.BlockSpec((1, d), lambda i, j: (0, 0)),
            pl.BlockSpec((None, d, tf), lambda i, j: (layer, 0, j)),
            pl.BlockSpec((None, d, tf), lambda i, j: (layer, 0, j)),
            pl.BlockSpec((None, tf, d), lambda i, j: (layer, j, 0)),
        ],
        out_specs=pl.BlockSpec((tm, d), lambda i, j: (i, 0)),
        out_shape=jax.ShapeDtypeStruct((s, d), F32),
        scratch_shapes=[pltpu.VMEM((tm, d), F32), pltpu.VMEM((tm, d), BF16), pltpu.SemaphoreType.DMA(())],
        compiler_params=_params("arbitrary", "arbitrary"),
        name="ffn",
    )(x, g, wg, wu, wd)


def _rope_body(pos_ref, invf_ref, cos_ref, sina_ref, sinb_ref):
    ang = pos_ref[...] * invf_ref[...]
    c = jnp.cos(ang)
    s = jnp.sin(ang)
    lane = lax.broadcasted_iota(jnp.int32, ang.shape, 1)
    first = (lane & (QK_HEAD_DIM // 2)) == 0
    cos_ref[...] = c
    sina_ref[...] = jnp.where(first, -s, 0.0)
    sinb_ref[...] = jnp.where(first, 0.0, s)


def _rope_tables(positions):
    s = positions.shape[0]
    inv_freq = 1.0 / (ROPE_THETA ** (jnp.arange(0, QK_HEAD_DIM, 2, dtype=F32) / QK_HEAD_DIM))
    invf = jnp.tile(inv_freq, 2 * V_HEAD_DIM // QK_HEAD_DIM).reshape(1, V_HEAD_DIM)
    pos = positions.astype(F32).reshape(s, 1)
    ts = min(1024, s)
    tab = jax.ShapeDtypeStruct((s, V_HEAD_DIM), F32)
    return pl.pallas_call(
        _rope_body,
        grid=(s // ts,),
        in_specs=[pl.BlockSpec((ts, 1), lambda i: (i, 0)), pl.BlockSpec((1, V_HEAD_DIM), lambda i: (0, 0))],
        out_specs=[pl.BlockSpec((ts, V_HEAD_DIM), lambda i: (i, 0))] * 3,
        out_shape=[tab, tab, tab],
        compiler_params=_params("parallel"),
        name="rope_tables",
    )(pos, invf)


def _group_mean_matrix():
    idx = np.arange(V_HEAD_DIM) // QK_HEAD_DIM
    return jnp.asarray((idx[:, None] == idx[None, :]).astype(np.float32) / QK_HEAD_DIM, dtype=BF16)


def _qk_norm_rope(t, gain, gmat, cos, sina, sinb):
    ms = jnp.dot((t * t).astype(BF16), gmat, preferred_element_type=F32)
    tn = t * lax.rsqrt(ms + NORM_EPS) * gain
    half = QK_HEAD_DIM // 2
    return tn * cos + pltpu.roll(tn, V_HEAD_DIM - half, 1) * sina + pltpu.roll(tn, half, 1) * sinb


def _in_proj_body(x_ref, g_ref, w_ref, gq_ref, gk_ref, gmat_ref, cos_ref, sina_ref, sinb_ref,
                  u_ref, qt_ref, k_ref, vt_ref, *, q_scale):
    n = N_HEADS * V_HEAD_DIM
    xn = _rms_rows(x_ref[...], g_ref[...]).astype(BF16)

    def group(c):
        return jnp.dot(xn, w_ref[:, c * n:(c + 1) * n], preferred_element_type=F32)

    u_ref[...] = group(0).astype(BF16)
    gmat, cos, sina, sinb = gmat_ref[...], cos_ref[...], sina_ref[...], sinb_ref[...]
    pq = group(1)
    for h in range(N_HEADS):
        t = _qk_norm_rope(pq[:, h * V_HEAD_DIM:(h + 1) * V_HEAD_DIM], gq_ref[...], gmat, cos, sina, sinb)
        qt_ref[h] = (t * q_scale).T.astype(BF16)
    pk = group(2)
    for h in range(N_HEADS):
        sl = slice(h * V_HEAD_DIM, (h + 1) * V_HEAD_DIM)
        k_ref[:, sl] = _qk_norm_rope(pk[:, sl], gk_ref[...], gmat, cos, sina, sinb).astype(BF16)
    pv = group(3)
    for h in range(N_HEADS):
        vt_ref[h, 0] = pv[:, h * V_HEAD_DIM:(h + 1) * V_HEAD_DIM].T.astype(BF16)


def _in_proj(x, g, w_in, layer, gq, gk, gmat, cos, sina, sinb, q_scale):
    s, d = x.shape
    n = N_HEADS * V_HEAD_DIM
    tm = min(ROW_TILE, s)
    assert tm == min(KV_TILE, s) and w_in.shape[-1] == 4 * n
    row_tab = pl.BlockSpec((tm, V_HEAD_DIM), lambda i: (i, 0))
    head_vec = pl.BlockSpec((1, V_HEAD_DIM), lambda i: (0, 0))
    return pl.pallas_call(
        functools.partial(_in_proj_body, q_scale=q_scale),
        grid=(s // tm,),
        in_specs=[
            pl.BlockSpec((tm, d), lambda i: (i, 0)),
            pl.BlockSpec((1, d), lambda i: (0, 0)),
            pl.BlockSpec((None, d, 4 * n), lambda i: (layer, 0, 0)),
            head_vec, head_vec,
            pl.BlockSpec((V_HEAD_DIM, V_HEAD_DIM), lambda i: (0, 0)),
            row_tab, row_tab, row_tab,
        ],
        out_specs=[
            pl.BlockSpec((tm, n), lambda i: (i, 0)),
            pl.BlockSpec((N_HEADS, V_HEAD_DIM, tm), lambda i: (0, 0, i)),
            pl.BlockSpec((tm, n), lambda i: (i, 0)),
            pl.BlockSpec((N_HEADS, 1, V_HEAD_DIM, tm), lambda i: (0, i, 0, 0)),
        ],
        out_shape=[
            jax.ShapeDtypeStruct((s, n), BF16),
            jax.ShapeDtypeStruct((N_HEADS, V_HEAD_DIM, s), BF16),
            jax.ShapeDtypeStruct((s, n), BF16),
            jax.ShapeDtypeStruct((N_HEADS, s // tm, V_HEAD_DIM, tm), BF16),
        ],
        compiler_params=_params("parallel"),
        name="in_proj",
    )(x, g, w_in, gq, gk, gmat, cos, sina, sinb)


def _dft_tables(s):
    n1, n2, tb = s // DFT_S2, DFT_S2, F1_S2_TILE
    a1 = 2.0 * np.pi * np.outer(np.arange(n1), np.arange(n1)) / n1
    cs1 = np.concatenate([np.cos(a1), -np.sin(a1)], axis=0)
    k1mat = np.kron(cs1, np.eye(tb))
    k = np.arange(n1)[:, None, None] + n1 * np.arange(n2)[None, :, None]
    a2 = 2.0 * np.pi * np.arange(n2)[None, None, :] * k / s
    c2, s2 = np.cos(a2), np.sin(a2)
    g2 = np.concatenate([np.concatenate([c2, s2], axis=2), np.concatenate([-s2, c2], axis=2)], axis=1)
    ac = 2.0 * np.pi * np.outer(np.arange(FOURIER_GROUP_DIM), np.arange(FOURIER_GROUP_DIM)) / FOURIER_GROUP_DIM
    cc = np.concatenate([np.cos(ac), np.sin(ac)], axis=0)
    return tuple(jnp.asarray(t, dtype=F32).astype(BF16) for t in (k1mat, g2, cc))


def _f1_body(u_ref, k_ref, z_ref):
    n1, tb, ncol = u_ref.shape
    z = jnp.dot(k_ref[...], u_ref[...].reshape(n1 * tb, ncol), preferred_element_type=F32)
    z_ref[...] = z.astype(BF16).reshape(2, n1, tb, ncol)


def _f2_body(z_ref, g_ref, cc_ref, y_ref, *, norm):
    gd = FOURIER_GROUP_DIM
    kb, n2, ncol = z_ref.shape[1], z_ref.shape[2], z_ref.shape[3]
    for kk in range(kb):
        zz = jnp.concatenate([z_ref[0, kk], z_ref[1, kk]], axis=0)
        xb = jnp.dot(g_ref[kk], zz, preferred_element_type=F32).astype(BF16)
        for g in range(ncol // gd):
            lhs = jnp.concatenate([xb[0:n2, g * gd:(g + 1) * gd], xb[n2:2 * n2, g * gd:(g + 1) * gd]], axis=1)
            y = jnp.dot(lhs, cc_ref[...], preferred_element_type=F32)
            y_ref[:, kk * ncol + g * gd:kk * ncol + (g + 1) * gd] = (y * norm).astype(BF16)


def _fourier_mix(u, tables):
    s, ncol = u.shape
    n1, n2 = s // DFT_S2, DFT_S2
    k1mat, g2, cc = tables
    tb = F1_S2_TILE
    z = pl.pallas_call(
        _f1_body,
        grid=(n2 // tb,),
        in_specs=[
            pl.BlockSpec((n1, tb, ncol), lambda t: (0, t, 0)),
            pl.BlockSpec((2 * n1 * tb, n1 * tb), lambda t: (0, 0)),
        ],
        out_specs=pl.BlockSpec((2, n1, tb, ncol), lambda t: (0, 0, t, 0)),
        out_shape=jax.ShapeDtypeStruct((2, n1, n2, ncol), BF16),
        compiler_params=_params("parallel"),
        name="dft_stage1",
    )(u.reshape(n1, n2, ncol), k1mat)
    norm = 1.0 / math.sqrt(s * FOURIER_GROUP_DIM)
    kb = math.gcd(n1, F2_K1_PER_STEP)
    y = pl.pallas_call(
        functools.partial(_f2_body, norm=norm),
        grid=(n1 // kb,),
        in_specs=[
            pl.BlockSpec((2, kb, n2, ncol), lambda k: (0, k, 0, 0)),
            pl.BlockSpec((kb, 2 * n2, 2 * n2), lambda k: (k, 0, 0)),
            pl.BlockSpec((2 * FOURIER_GROUP_DIM, FOURIER_GROUP_DIM), lambda k: (0, 0)),
        ],
        out_specs=pl.BlockSpec((n2, kb * ncol), lambda k: (0, k)),
        out_shape=jax.ShapeDtypeStruct((n2, n1 * ncol), BF16),
        compiler_params=_params("parallel"),
        name="dft_stage2",
    )(z, g2, cc)
    return y.reshape(s, ncol)


def _score_bound(gq, gk):
    bq = jnp.max(jnp.abs(gq), axis=-1, keepdims=True)
    bk = jnp.max(jnp.abs(gk), axis=-1, keepdims=True)
    return (QK_HEAD_DIM * (QK_HEAD_DIM ** -0.5) * LOG2_E) * bq * bk


def _for_each_q_tile(qt_ref, o_ref, tile_fn):
    tq = min(Q_TILE, qt_ref.shape[2])

    def sub(i, carry):
        off = pl.multiple_of(i * tq, tq)
        tile_fn(qt_ref[0, :, pl.ds(off, tq)], o_ref.at[pl.ds(off, tq), :])
        return carry

    lax.fori_loop(0, qt_ref.shape[2] // tq, sub, 0)


def _attn_prologue(qt):
    row = lax.broadcasted_iota(jnp.int32, qt.shape, 0)
    zero = jnp.zeros_like(qt)
    return jnp.where(row < QK_HEAD_DIM, qt, zero), jnp.where(row >= QK_HEAD_DIM, qt, zero)


def _attn_epilogue(o1, o2, lq1_ref, lk1_ref, lq2_ref, lk2_ref, gsub_ref, o_ref, lambda_init):
    lam = (jnp.exp(jnp.sum(lq1_ref[...] * lk1_ref[...], axis=-1, keepdims=True))
           - jnp.exp(jnp.sum(lq2_ref[...] * lk2_ref[...], axis=-1, keepdims=True)) + lambda_init)
    o = o1 - lam * o2
    ms = jnp.mean(o * o, axis=0, keepdims=True)
    y = o * lax.rsqrt(ms + SUBLN_EPS) * gsub_ref[...] * (1.0 - lambda_init)
    o_ref[...] = y.T.astype(BF16)


def _attn_online_body(qt_ref, k_ref, vt_ref, gq_ref, gk_ref, lq1_ref, lk1_ref, lq2_ref, lk2_ref, gsub_ref, o_ref,
                      m_ref, l_ref, acc_ref, *, lambda_init):
    del gq_ref, gk_ref
    nk, tk = vt_ref.shape[1], vt_ref.shape[3]

    def tile(qt, o_view):
        q_halves = _attn_prologue(qt)
        m_ref[...] = jnp.full(m_ref.shape, M_INIT, F32)
        l_ref[...] = jnp.zeros(l_ref.shape, F32)
        acc_ref[...] = jnp.zeros(acc_ref.shape, F32)

        def chunk(c, carry):
            off = pl.multiple_of(c * tk, tk)
            kc = k_ref[pl.ds(off, tk), :]
            vc = vt_ref[0, c]
            for hf in range(2):
                s = jnp.dot(kc, q_halves[hf], preferred_element_type=F32)
                m_old = m_ref[hf]
                m_new = jnp.maximum(m_old, jnp.max(s, axis=0, keepdims=True))
                alpha = jnp.exp2(m_old - m_new)
                p = jnp.exp2(s - m_new)
                l_ref[hf] = alpha * l_ref[hf] + jnp.sum(p, axis=0, keepdims=True)
                acc_ref[hf] = alpha * acc_ref[hf] + jnp.dot(vc, p.astype(BF16), preferred_element_type=F32)
                m_ref[hf] = m_new
            return carry

        lax.fori_loop(0, nk, chunk, 0)
        _attn_epilogue(acc_ref[0] / l_ref[0], acc_ref[1] / l_ref[1],
                       lq1_ref, lk1_ref, lq2_ref, lk2_ref, gsub_ref, o_view, lambda_init)

    _for_each_q_tile(qt_ref, o_ref, tile)


def _attn_bounded_body(qt_ref, k_ref, vt_ref, gq_ref, gk_ref, lq1_ref, lk1_ref, lq2_ref, lk2_ref, gsub_ref, o_ref,
                       l_ref, acc_ref, *, lambda_init):
    nk, tk = vt_ref.shape[1], vt_ref.shape[3]
    bound = _score_bound(gq_ref[...], gk_ref[...])

    per_chunk = tk // KV_UNIT
    unroll = math.gcd(nk, KV_UNROLL)
    units = [(cc, j, hf) for cc in range(unroll) for j in range(per_chunk) for hf in range(2)]

    tq = min(Q_TILE, qt_ref.shape[2])
    n_tiles = qt_ref.shape[2] // tq

    def q_rows(i):
        return pl.ds(i * tq if isinstance(i, int) else pl.multiple_of(i * tq, tq), tq)

    def accumulate(i):
        q_halves = _attn_prologue(qt_ref[0, :, q_rows(i)])

        def trip(t, first_trip):
            def scores(cc, j, hf):
                off = pl.multiple_of((t * unroll + cc) * tk + j * KV_UNIT, KV_UNIT)
                return jnp.dot(k_ref[pl.ds(off, KV_UNIT), :], q_halves[hf], preferred_element_type=F32)

            pending = [scores(*u) for u in units[:QK_LOOKAHEAD]]
            for n, (cc, j, hf) in enumerate(units):
                s = pending.pop(0)
                if n + QK_LOOKAHEAD < len(units):
                    pending.append(scores(*units[n + QK_LOOKAHEAD]))
                p = jnp.exp2(s - bound)
                psum = jnp.sum(p.reshape(KV_UNIT // 8, 8, tq), axis=0)
                vc = vt_ref[0, t * unroll + cc, :, j * KV_UNIT:(j + 1) * KV_UNIT]
                pv = jnp.dot(vc, p.astype(BF16), preferred_element_type=F32)
                if first_trip and n < 2:
                    l_ref[hf] = psum
                    acc_ref[hf] = pv
                else:
                    l_ref[hf] += psum
                    acc_ref[hf] += pv

        trip(0, True)
        if nk // unroll > 1:
            lax.fori_loop(1, nk // unroll, lambda t, c: (trip(t, False), c)[1], 0)

    def finish(i):
        l1 = jnp.sum(l_ref[0], axis=0, keepdims=True)
        l2 = jnp.sum(l_ref[1], axis=0, keepdims=True)
        _attn_epilogue(acc_ref[0] / l1, acc_ref[1] / l2, lq1_ref, lk1_ref, lq2_ref, lk2_ref, gsub_ref,
                       o_ref.at[q_rows(i), :], lambda_init)

    accumulate(0)

    def step(i, carry):
        finish(i - 1)
        accumulate(i)
        return carry

    lax.fori_loop(1, n_tiles, step, 0)
    finish(n_tiles - 1)


def _diff_attention(qt, k, vt, gq, gk, lq1, lk1, lq2, lk2, gsub, lambda_init, online):
    h, _, s = qt.shape
    nk, tk = vt.shape[1], vt.shape[3]
    tq = min(Q_TILE, s)
    tqb = min(tq * Q_TILES_PER_STEP, s)
    vec = pl.BlockSpec((1, QK_HEAD_DIM), lambda hh, i: (0, 0))
    if online:
        body = _attn_online_body
        scratch = [pltpu.VMEM((2, 1, tq), F32), pltpu.VMEM((2, 1, tq), F32), pltpu.VMEM((2, V_HEAD_DIM, tq), F32)]
    else:
        body = _attn_bounded_body
        scratch = [pltpu.VMEM((2, 8, tq), F32), pltpu.VMEM((2, V_HEAD_DIM, tq), F32)]
    return pl.pallas_call(
        functools.partial(body, lambda_init=lambda_init),
        grid=(h, s // tqb),
        in_specs=[
            pl.BlockSpec((1, V_HEAD_DIM, tqb), lambda hh, i: (hh, 0, i)),
            pl.BlockSpec((s, V_HEAD_DIM), lambda hh, i: (0, hh)),
            pl.BlockSpec((1, nk, V_HEAD_DIM, tk), lambda hh, i: (hh, 0, 0, 0)),
            vec, vec, vec, vec, vec, vec,
            pl.BlockSpec((V_HEAD_DIM, 1), lambda hh, i: (0, 0)),
        ],
        out_specs=pl.BlockSpec((tqb, V_HEAD_DIM), lambda hh, i: (i, hh)),
        out_shape=jax.ShapeDtypeStruct((s, h * V_HEAD_DIM), BF16),
        scratch_shapes=scratch,
        compiler_params=_params("parallel", "parallel"),
        name="diff_attention_online" if online else "diff_attention_bounded",
    )(qt, k, vt, gq, gk, lq1, lk1, lq2, lk2, gsub)


def _outproj_body(x_ref, yf_ref, ya_ref, wf_ref, wa_ref, o_ref, w_bf16_ref):
    @pl.when(pl.program_id(0) == 0)
    def _():
        w_bf16_ref[0] = wf_ref[...].astype(BF16)
        w_bf16_ref[1] = wa_ref[...].astype(BF16)

    o_ref[...] = (x_ref[...]
                  + jnp.dot(yf_ref[...], w_bf16_ref[0], preferred_element_type=F32)
                  + jnp.dot(ya_ref[...], w_bf16_ref[1], preferred_element_type=F32))


def _outproj(x, yf, ya, w_out, layer):
    s, d = x.shape
    nf, na = yf.shape[1], ya.shape[1]
    assert nf == na
    tm = min(ROW_TILE, s)
    return pl.pallas_call(
        _outproj_body,
        grid=(s // tm,),
        in_specs=[
            pl.BlockSpec((tm, d), lambda i: (i, 0)),
            pl.BlockSpec((tm, nf), lambda i: (i, 0)),
            pl.BlockSpec((tm, na), lambda i: (i, 0)),
            pl.BlockSpec((None, nf, d), lambda i: (layer, 0, 0), pipeline_mode=pl.Buffered(1)),
            pl.BlockSpec((None, na, d), lambda i: (layer, 1, 0), pipeline_mode=pl.Buffered(1)),
        ],
        out_specs=pl.BlockSpec((tm, d), lambda i: (i, 0)),
        out_shape=jax.ShapeDtypeStruct((s, d), F32),
        scratch_shapes=[pltpu.VMEM((2, nf, d), BF16)],
        compiler_params=_params("arbitrary"),
        name="out_proj",
    )(x, yf, ya, w_out, w_out)


def kernel(x, positions, norm_ffn1, w1_gate, w1_up, w1_down, norm_mix, w_in, q_norm, k_norm,
           lambda_q1, lambda_k1, lambda_q2, lambda_k2, subln, w_out, norm_ffn2, w2_gate, w2_up, w2_down):
    b, s, d = x.shape
    depth = w_in.shape[0]
    assert b == 1 and s % DFT_S2 == 0
    xs = x.reshape(s, d)
    win = w_in.astype(BF16)

    cos, sina, sinb = _rope_tables(positions.reshape(s))
    gmat = _group_mean_matrix()
    tables = _dft_tables(s)
    q_scale = (QK_HEAD_DIM ** -0.5) * LOG2_E

    for l in range(depth):
        lambda_init = 0.8 - 0.6 * math.exp(-0.3 * l)
        xs = _ffn(xs, norm_ffn1[l].reshape(1, d), w1_gate, w1_up, w1_down, l)

        g_mix = norm_mix[l].reshape(1, d)
        gq = jnp.tile(q_norm[l], 2).reshape(1, V_HEAD_DIM)
        gk = jnp.tile(k_norm[l], 2).reshape(1, V_HEAD_DIM)
        u_f, qt, kk, vt = _in_proj(xs, g_mix, win, l, gq, gk, gmat, cos, sina, sinb, q_scale)
        y_f = _fourier_mix(u_f, tables)
        gq64, gk64 = q_norm[l].reshape(1, QK_HEAD_DIM), k_norm[l].reshape(1, QK_HEAD_DIM)
        attn_args = (qt, kk, vt, gq64, gk64,
                     lambda_q1[l].reshape(1, -1), lambda_k1[l].reshape(1, -1),
                     lambda_q2[l].reshape(1, -1), lambda_k2[l].reshape(1, -1),
                     subln[l].reshape(V_HEAD_DIM, 1))
        y_a = lax.cond(
            _score_bound(gq64, gk64)[0, 0] <= MAX_BOUNDED_SOFTMAX_LOG2,
            lambda *a: _diff_attention(*a, lambda_init, False),
            lambda *a: _diff_attention(*a, lambda_init, True),
            *attn_args)
        xs = _outproj(xs, y_f, y_a, w_out, l)

        xs = _ffn(xs, norm_ffn2[l].reshape(1, d), w2_gate, w2_up, w2_down, l)
    return xs.reshape(b, s, d)
```
